```python
import math, functools
import jax, jax.numpy as jnp
from jax import lax
import numpy as np

D_MODEL = 1024
BATCH = 4
SEQ = 4096
DEPTH = 4
DEC_BATCH = 32
DEC_SEQ = 8
PAST_LEN = 8192
PAGE_SIZE = 128

GLA_HEADS = 4
GLA_DK = 64
GLA_DV = 128
GLA_QK = GLA_HEADS * GLA_DK
GLA_V = GLA_HEADS * GLA_DV
GLA_RANK = 16
GLA_TAU = 16.0
GLA_CHUNK = 64
MOBA_HEADS = 8
MOBA_DH = 64
MOBA_W = MOBA_HEADS * MOBA_DH
MOBA_BLOCK = 256
MOBA_TOPK = 3
MOBA_QCHUNK = 32
SG_GROUPS = 4
SG_GC = 128
SG_W = SG_GROUPS * SG_GC
SG_CHUNK = 128
N_EXPERTS = 64
TOP_K = 8
N_EXPERT_GROUPS = 8
TOPK_GROUPS = 4
D_EXPERT = 128
D_SHARED = 128
ROUTED_SCALE = 2.5
MOE_TOKEN_BLOCK = 1024
ALPHA = (2 * DEPTH) ** 0.25
BETA = (8 * DEPTH) ** -0.25
N_IN = 2 * GLA_QK + 2 * GLA_V + GLA_RANK + 3 * MOBA_W + 2 * SG_W + 3 * D_MODEL
LN_EPS = 1e-5

kernel_name = 'hybrid_gla_moba_sgu_moe_decode_step'


def _layer_norm(x, g, b):
    xf = x.astype(jnp.float32)
    mu = jnp.mean(xf, -1, keepdims=True)
    var = jnp.mean(jnp.square(xf - mu), -1, keepdims=True)
    return ((xf - mu) * lax.rsqrt(var + LN_EPS) * g.astype(jnp.float32) + b.astype(jnp.float32)).astype(x.dtype)


def _split_in(h):
    sizes = (GLA_QK, GLA_QK, GLA_V, GLA_V, GLA_RANK, MOBA_W, MOBA_W, MOBA_W, SG_W, SG_W, 3 * D_MODEL)
    cuts = [int(c) for c in np.cumsum(sizes)[:-1]]
    return jnp.split(h, cuts, axis=-1)


def _gla_recurrence(q, k, v, log_a, s0):
    B, T, H, DK = q.shape
    C = math.gcd(T, GLA_CHUNK)
    n = T // C

    def to_chunks(t):
        return t.reshape(B, n, C, H, t.shape[-1]).transpose(1, 0, 3, 2, 4)

    causal = jnp.tril(jnp.ones((C, C), bool))

    def step(S, inp):
        qi, ki, vi, ai = inp
        b = jnp.cumsum(ai, axis=2)
        inter = jnp.einsum('bhtk,bhkv->bhtv', qi * jnp.exp(b), S)
        diff = b[:, :, :, None, :] - b[:, :, None, :, :]
        decay = jnp.exp(jnp.where(causal[:, :, None], diff, -jnp.inf))
        att = jnp.einsum('bhtk,bhsk,bhtsk->bhts', qi, ki, decay)
        intra = jnp.einsum('bhts,bhsv->bhtv', att, vi)
        b_last = b[:, :, -1:, :]
        S_new = jnp.exp(b_last[:, :, 0, :, None]) * S + jnp.einsum('bhsk,bhsv->bhkv', ki * jnp.exp(b_last - b), vi)
        return S_new, inter + intra

    s_fin, out = lax.scan(step, s0, (to_chunks(q), to_chunks(k), to_chunks(v), to_chunks(log_a)))
    out = out.transpose(1, 0, 3, 2, 4).reshape(B, T, H, v.shape[-1])
    return out, s_fin


def _gla_branch(q, k, v, r, a_lr, w_a2, b_a2, norm_g, s0):
    B, T, _ = q.shape
    f32 = jnp.float32
    qh = (q.astype(f32) * GLA_DK ** -0.5).reshape(B, T, GLA_HEADS, GLA_DK)
    kh = k.astype(f32).reshape(B, T, GLA_HEADS, GLA_DK)
    vh = v.astype(f32).reshape(B, T, GLA_HEADS, GLA_DV)
    log_a = jax.nn.log_sigmoid((a_lr @ w_a2 + b_a2).astype(f32)).reshape(B, T, GLA_HEADS, GLA_DK) / GLA_TAU
    o, s_fin = _gla_recurrence(qh, kh, vh, log_a, s0.astype(f32))
    o = o * lax.rsqrt(jnp.mean(jnp.square(o), -1, keepdims=True) + 1e-6) * norm_g.astype(f32)
    o = o.reshape(B, T, GLA_V) * jax.nn.silu(r.astype(f32))
    return o.astype(q.dtype), s_fin.astype(q.dtype)


def _moba_prompt(q, k, v):
    B, S, H, Dh = q.shape
    f32 = jnp.float32
    nb = -(-S // MOBA_BLOCK)
    pad = nb * MOBA_BLOCK - S
    kk = min(MOBA_TOPK, (S - 1) // MOBA_BLOCK)
    qh = q.transpose(0, 2, 1, 3).astype(f32) * Dh ** -0.5
    kh = jnp.pad(k.transpose(0, 2, 1, 3), ((0, 0), (0, 0), (0, pad), (0, 0)))
    vh = jnp.pad(v.transpose(0, 2, 1, 3), ((0, 0), (0, 0), (0, pad), (0, 0)))
    kblk = kh.reshape(B, H, nb, MOBA_BLOCK, Dh)
    vblk = vh.reshape(B, H, nb, MOBA_BLOCK, Dh)
    kmean = jnp.mean(kblk.astype(f32), axis=3)
    b_i = jnp.arange(B)[:, None, None, None]
    h_i = jnp.arange(H)[None, :, None, None]
    QC = MOBA_QCHUNK

    def chunk(c):
        q0 = c * QC
        qc = lax.dynamic_slice_in_dim(qh, q0, QC, axis=2)
        qpos = q0 + jnp.arange(QC)
        own = q0 // MOBA_BLOCK
        k_own = lax.dynamic_slice_in_dim(kh, own * MOBA_BLOCK, MOBA_BLOCK, axis=2).astype(f32)
        v_own = lax.dynamic_slice_in_dim(vh, own * MOBA_BLOCK, MOBA_BLOCK, axis=2).astype(f32)
        kpos = own * MOBA_BLOCK + jnp.arange(MOBA_BLOCK)
        s_own = jnp.einsum('bhqd,bhkd->bhqk', qc, k_own)
        s_own = jnp.where(kpos[None, :] <= qpos[:, None], s_own, -jnp.inf)
        if kk == 0:
            p = jax.nn.softmax(s_own, axis=-1)
            return jnp.einsum('bhqk,bhkd->bhqd', p, v_own)
        gate = jnp.einsum('bhqd,bhnd->bhqn', qc, kmean)
        gate = jnp.where(jnp.arange(nb) < own, gate, -jnp.inf)
        _, idx = lax.top_k(gate, kk)
        valid = idx < own
        k_sel = kblk[b_i, h_i, idx].astype(f32)
        v_sel = vblk[b_i, h_i, idx].astype(f32)
        s_sel = jnp.einsum('bhqd,bhqjkd->bhqjk', qc, k_sel)
        s_sel = jnp.where(valid[..., None], s_sel, -jnp.inf).reshape(B, H, QC, kk * MOBA_BLOCK)
        p = jax.nn.softmax(jnp.concatenate([s_sel, s_own], axis=-1), axis=-1)
        p_sel = p[..., :kk * MOBA_BLOCK].reshape(B, H, QC, kk, MOBA_BLOCK)
        p_own = p[..., kk * MOBA_BLOCK:]
        return jnp.einsum('bhqjk,bhqjkd->bhqd', p_sel, v_sel) + jnp.einsum('bhqk,bhkd->bhqd', p_own, v_own)

    out = lax.map(chunk, jnp.arange(S // QC))
    return out.transpose(1, 0, 3, 2, 4).reshape(B, S, H, Dh).astype(q.dtype)


def _moba_sample(q, k, v, cache_k, cache_v, page_table, layer):
    DB, T, H, Dh = q.shape
    f32 = jnp.float32
    past = page_table.shape[1] * PAGE_SIZE
    ppb = MOBA_BLOCK // PAGE_SIZE
    n_full = past // MOBA_BLOCK
    kk = min(MOBA_TOPK, n_full)
    own_pages = (past - n_full * MOBA_BLOCK) // PAGE_SIZE
    qh = q.transpose(0, 2, 1, 3).astype(f32) * Dh ** -0.5
    k_new = k.transpose(0, 2, 1, 3).astype(f32)
    v_new = v.transpose(0, 2, 1, 3).astype(f32)
    h3 = jnp.arange(H)[None, :, None]
    scores, values = [], []
    if kk > 0:
        pt_full = page_table[:, None, :n_full * ppb]
        k_past = cache_k[layer, pt_full, h3].reshape(DB, H, n_full, MOBA_BLOCK, Dh)
        kmean = jnp.mean(k_past.astype(f32), axis=3)
        gate = jnp.einsum('bhqd,bhnd->bhqn', qh, kmean)
        _, idx = lax.top_k(gate, kk)
        b4 = jnp.arange(DB)[:, None, None, None]
        h4 = jnp.arange(H)[None, :, None, None]
        k_sel = k_past[b4, h4, idx].astype(f32).reshape(DB, H, T, kk * MOBA_BLOCK, Dh)
        logical = idx[..., None] * ppb + jnp.arange(ppb)
        phys = page_table[jnp.arange(DB)[:, None, None, None, None], logical]
        v_sel = cache_v[layer, phys, jnp.arange(H)[None, :, None, None, None]]
        v_sel = v_sel.astype(f32).reshape(DB, H, T, kk * MOBA_BLOCK, Dh)
        scores.append(jnp.einsum('bhqd,bhqkd->bhqk', qh, k_sel))
        values.append(v_sel)
    if own_pages > 0:
        pt_own = page_table[:, None, n_full * ppb:]
        k_own = cache_k[layer, pt_own, h3].reshape(DB, H, own_pages * PAGE_SIZE, Dh).astype(f32)
        v_own = cache_v[layer, pt_own, h3].reshape(DB, H, own_pages * PAGE_SIZE, Dh).astype(f32)
        scores.append(jnp.einsum('bhqd,bhkd->bhqk', qh, k_own))
        values.append(v_own)
    s_new = jnp.einsum('bhqd,bhkd->bhqk', qh, k_new)
    scores.append(jnp.where(jnp.tril(jnp.ones((T, T), bool)), s_new, -jnp.inf))
    values.append(v_new)
    p = jax.nn.softmax(jnp.concatenate(scores, axis=-1), axis=-1)
    out = jnp.zeros((DB, H, T, Dh), f32)
    off = 0
    for s_part, v_part in zip(scores, values):
        n = s_part.shape[-1]
        pp = p[..., off:off + n]
        off += n
        if v_part.ndim == 5:
            out = out + jnp.einsum('bhqk,bhqkd->bhqd', pp, v_part)
        else:
            out = out + jnp.einsum('bhqk,bhkd->bhqd', pp, v_part)
    return out.transpose(0, 2, 1, 3).astype(q.dtype)


def _sgu_branch(u, v, ln_g, ln_b, w_s, b_s):
    B, T, _ = u.shape
    f32 = jnp.float32
    C = min(T, SG_CHUNK)
    n = T // C
    u = jax.nn.gelu(u.astype(f32), approximate=False)
    v = _layer_norm(jax.nn.gelu(v.astype(f32), approximate=False), ln_g, ln_b)
    w = jnp.tril(w_s[:, :C, :C].astype(f32))
    vg = v.reshape(B, n, C, SG_GROUPS, SG_GC)
    mix = jnp.einsum('gts,bnsgc->bntgc', w, vg) + b_s[:, :C].astype(f32).T[None, None, :, :, None]
    out = u * mix.reshape(B, T, SG_W)
    return out.astype(ln_g.dtype), v.astype(ln_g.dtype)


def _moe(x, lw):
    B, T, D = x.shape
    n = B * T
    f32 = jnp.float32
    xt = x.reshape(n, D)
    s = jax.nn.sigmoid((xt @ lw['moe_w_router']).astype(f32))
    sel = s + lw['moe_b_router'].astype(f32)
    grp_score = jnp.sum(lax.top_k(sel.reshape(n, N_EXPERT_GROUPS, N_EXPERTS // N_EXPERT_GROUPS), 2)[0], -1)
    _, gidx = lax.top_k(grp_score, TOPK_GROUPS)
    gmask = jnp.any(gidx[:, :, None] == jnp.arange(N_EXPERT_GROUPS), axis=1)
    emask = jnp.repeat(gmask, N_EXPERTS // N_EXPERT_GROUPS, axis=1)
    _, eidx = lax.top_k(jnp.where(emask, sel, -jnp.inf), TOP_K)
    w = jnp.take_along_axis(s, eidx, axis=1)
    w = w / jnp.sum(w, -1, keepdims=True) * ROUTED_SCALE
    gates = jnp.sum(jax.nn.one_hot(eidx, N_EXPERTS, dtype=f32) * w[..., None], axis=1)
    blk = math.gcd(n, MOE_TOKEN_BLOCK)

    def expert_block(args):
        xb, gb = args
        h = jax.nn.silu(jnp.einsum('td,edf->tef', xb, lw['moe_w_gate'])) * jnp.einsum('td,edf->tef', xb, lw['moe_w_up'])
        return jnp.einsum('tef,efd->td', h * gb[:, :, None].astype(h.dtype), lw['moe_w_down'])

    routed = lax.map(expert_block, (xt.reshape(n // blk, blk, D), gates.reshape(n // blk, blk, N_EXPERTS))).reshape(n, D)
    shared = (jax.nn.silu(xt @ lw['sh_w_gate']) * (xt @ lw['sh_w_up'])) @ lw['sh_w_down']
    return (routed + shared).reshape(B, T, D)


def _trunk_layer(x, lw, gla_s0, attend):
    B, T, _ = x.shape
    qa, ka, va, ra, alr, qb, kb, vb, uc, vc, gl = _split_in(x @ lw['w_in'])
    o_a, s_a = _gla_branch(qa, ka, va, ra, alr, lw['gla_w_a2'], lw['gla_b_a2'], lw['gla_norm_g'], gla_s0)
    qb = qb.reshape(B, T, MOBA_HEADS, MOBA_DH)
    kb = kb.reshape(B, T, MOBA_HEADS, MOBA_DH)
    vb = vb.reshape(B, T, MOBA_HEADS, MOBA_DH)
    o_b = attend(qb, kb, vb).reshape(B, T, MOBA_W)
    o_c, v_c = _sgu_branch(uc, vc, lw['sg_ln_g'], lw['sg_ln_b'], lw['sg_w'], lw['sg_b'])
    g_a, g_b, g_c = jnp.split(jax.nn.sigmoid(gl.astype(jnp.float32)).astype(x.dtype), 3, axis=-1)
    merged = g_a * (o_a @ lw['w_branch_a']) + g_b * (o_b @ lw['w_branch_b']) + g_c * (o_c @ lw['w_branch_c'])
    x = _layer_norm(ALPHA * x + merged @ lw['w_out'], lw['ln1_g'], lw['ln1_b'])
    x = _layer_norm(ALPHA * x + _moe(x, lw), lw['ln2_g'], lw['ln2_b'])
    return x, s_a, kb, vb, v_c


def setup_inputs(seed: int = 0) -> dict:
    key = jax.random.key(seed)
    ks = jax.random.split(key, 32)
    f32 = jnp.float32

    def nrm(i, shape, scale):
        return jax.random.normal(ks[i], shape, f32) * scale

    n_pages = PAST_LEN // PAGE_SIZE
    n_used = DEC_BATCH * n_pages
    n_pool = n_used + (n_used + 3) // 4
    page_table = jax.random.permutation(ks[5], n_pool)[:n_used].reshape(DEC_BATCH, n_pages).astype(jnp.int32)
    D, E, F, FS = D_MODEL, N_EXPERTS, D_EXPERT, D_SHARED
    return {
        'x_prompt': nrm(0, (BATCH, SEQ, D), 1.0),
        'x_sample': nrm(1, (DEC_BATCH, DEC_SEQ, D), 1.0),
        'cache_k': nrm(2, (DEPTH, n_pool, MOBA_HEADS, PAGE_SIZE, MOBA_DH), 1.0),
        'cache_v': nrm(3, (DEPTH, n_pool, MOBA_HEADS, PAGE_SIZE, MOBA_DH), 1.0),
        'state_gla': nrm(4, (DEPTH, DEC_BATCH, GLA_HEADS, GLA_DK, GLA_DV), 1.0),
        'page_table': page_table,
        'w_in': nrm(6, (DEPTH, D, N_IN), D ** -0.5),
        'gla_w_a2': nrm(7, (DEPTH, GLA_RANK, GLA_QK), GLA_RANK ** -0.5),
        'gla_b_a2': nrm(8, (DEPTH, GLA_QK), 0.01),
        'gla_norm_g': 1.0 + nrm(9, (DEPTH, GLA_DV), 0.02),
        'sg_ln_g': 1.0 + nrm(10, (DEPTH, SG_W), 0.02),
        'sg_ln_b': nrm(11, (DEPTH, SG_W), 0.02),
        'sg_w': nrm(12, (DEPTH, SG_GROUPS, SG_CHUNK, SG_CHUNK), SG_CHUNK ** -0.5),
        'sg_b': 1.0 + nrm(13, (DEPTH, SG_GROUPS, SG_CHUNK), 0.01),
        'w_branch_a': nrm(14, (DEPTH, GLA_V, D), GLA_V ** -0.5),
        'w_branch_b': nrm(15, (DEPTH, MOBA_W, D), MOBA_W ** -0.5),
        'w_branch_c': nrm(16, (DEPTH, SG_W, D), SG_W ** -0.5),
        'w_out': nrm(17, (DEPTH, D, D), D ** -0.5 * BETA),
        'ln1_g': 1.0 + nrm(18, (DEPTH, D), 0.02),
        'ln1_b': nrm(19, (DEPTH, D), 0.02),
        'ln2_g': 1.0 + nrm(20, (DEPTH, D), 0.02),
        'ln2_b': nrm(21, (DEPTH, D), 0.02),
        'moe_w_router': nrm(22, (DEPTH, D, E), D ** -0.5),
        'moe_b_router': nrm(23, (DEPTH, E), 0.01),
        'moe_w_gate': nrm(24, (DEPTH, E, D, F), D ** -0.5),
        'moe_w_up': nrm(25, (DEPTH, E, D, F), D ** -0.5),
        'moe_w_down': nrm(26, (DEPTH, E, F, D), F ** -0.5 * BETA),
        'sh_w_gate': nrm(27, (DEPTH, D, FS), D ** -0.5),
        'sh_w_up': nrm(28, (DEPTH, D, FS), D ** -0.5),
        'sh_w_down': nrm(29, (DEPTH, FS, D), FS ** -0.5 * BETA),
    }


def reference(x_prompt, x_sample, cache_k, cache_v, state_gla, page_table, w_in, gla_w_a2, gla_b_a2, gla_norm_g,
              sg_ln_g, sg_ln_b, sg_w, sg_b, w_branch_a, w_branch_b, w_branch_c, w_out, ln1_g, ln1_b, ln2_g, ln2_b,
              moe_w_router, moe_b_router, moe_w_gate, moe_w_up, moe_w_down, sh_w_gate, sh_w_up, sh_w_down):
    yp, ys = x_prompt, x_sample
    Bp, Sp, _ = x_prompt.shape
    DB, T, _ = x_sample.shape
    gla_p, gla_s, kp_l, vp_l, ks_l, vs_l, sgu_l = [], [], [], [], [], [], []
    for l in range(DEPTH):
        lw = {
            'w_in': w_in[l], 'gla_w_a2': gla_w_a2[l], 'gla_b_a2': gla_b_a2[l], 'gla_norm_g': gla_norm_g[l],
            'sg_ln_g': sg_ln_g[l], 'sg_ln_b': sg_ln_b[l], 'sg_w': sg_w[l], 'sg_b': sg_b[l],
            'w_branch_a': w_branch_a[l], 'w_branch_b': w_branch_b[l], 'w_branch_c': w_branch_c[l], 'w_out': w_out[l],
            'ln1_g': ln1_g[l], 'ln1_b': ln1_b[l], 'ln2_g': ln2_g[l], 'ln2_b': ln2_b[l],
            'moe_w_router': moe_w_router[l], 'moe_b_router': moe_b_router[l], 'moe_w_gate': moe_w_gate[l],
            'moe_w_up': moe_w_up[l], 'moe_w_down': moe_w_down[l], 'sh_w_gate': sh_w_gate[l], 'sh_w_up': sh_w_up[l],
            'sh_w_down': sh_w_down[l],
        }
        s0p = jnp.zeros((Bp, GLA_HEADS, GLA_DK, GLA_DV), jnp.float32)
        yp, sp, kp, vp, _ = _trunk_layer(yp, lw, s0p, _moba_prompt)
        attend_s = functools.partial(_moba_sample, cache_k=cache_k, cache_v=cache_v, page_table=page_table, layer=l)
        ys, ss, kss, vss, vcs = _trunk_layer(ys, lw, state_gla[l], attend_s)
        gla_p.append(sp)
        gla_s.append(ss)
        kp_l.append(kp.reshape(Bp, Sp // PAGE_SIZE, PAGE_SIZE, MOBA_HEADS, MOBA_DH).transpose(0, 1, 3, 2, 4))
        vp_l.append(vp.reshape(Bp, Sp // PAGE_SIZE, PAGE_SIZE, MOBA_HEADS, MOBA_DH).transpose(0, 1, 3, 2, 4))
        ks_l.append(kss.transpose(0, 2, 1, 3))
        vs_l.append(vss.transpose(0, 2, 1, 3))
        sgu_l.append(vcs)
    new_state_gla_prompt = jnp.stack(gla_p)
    new_state_gla_sample = jnp.stack(gla_s)
    new_cache_k_prompt = jnp.stack(kp_l)
    new_cache_v_prompt = jnp.stack(vp_l)
    new_cache_k_sample = jnp.stack(ks_l)
    new_cache_v_sample = jnp.stack(vs_l)
    new_state_sgu_v_sample = jnp.stack(sgu_l)
    return (yp, ys, new_state_gla_prompt, new_state_gla_sample, new_cache_k_prompt, new_cache_v_prompt, new_cache_k_sample, new_cache_v_sample, new_state_sgu_v_sample)
```

```python
import functools
import math

import numpy as np
import jax
import jax.numpy as jnp
from jax import lax
from jax.experimental import pallas as pl
from jax.experimental.pallas import tpu as pltpu

F32 = jnp.float32
BF16 = jnp.bfloat16

D_MODEL = 1024
DEPTH = 4
PAGE_SIZE = 128
GLA_HEADS, GLA_DK, GLA_DV, GLA_RANK, GLA_TAU = 4, 64, 128, 16, 16.0
GLA_QK, GLA_V = GLA_HEADS * GLA_DK, GLA_HEADS * GLA_DV
MOBA_HEADS, MOBA_DH, MOBA_BLOCK, MOBA_TOPK = 8, 64, 256, 3
MOBA_W = MOBA_HEADS * MOBA_DH
SG_GROUPS, SG_GC, SG_CHUNK = 4, 128, 128
SG_W = SG_GROUPS * SG_GC
N_EXPERTS, TOP_K, N_EXPERT_GROUPS, TOPK_GROUPS = 64, 8, 8, 4
D_EXPERT, D_SHARED, ROUTED_SCALE = 128, 128, 2.5
ALPHA = (2 * DEPTH) ** 0.25
LN_EPS = 1e-5

LANES = 128
SUBLANES = 8
VMEM_LIMIT_BYTES = 56 * 1024 * 1024

N_IN_PERM = 2 * GLA_QK + 2 * GLA_V + 3 * MOBA_W + 2 * SG_W + 3 * D_MODEL + LANES
COL_QB, COL_KB, COL_VB = 12, 16, 20
COL_ALR = 56
GLA_SUB = 16
NEG_INF = float("-inf")


def _cparams(sem):
    return pltpu.CompilerParams(dimension_semantics=sem, vmem_limit_bytes=VMEM_LIMIT_BYTES)


def _layer_norm(x, g, b):
    mu = jnp.mean(x, -1, keepdims=True)
    xc = x - mu
    var = jnp.mean(xc * xc, -1, keepdims=True)
    return xc * lax.rsqrt(var + LN_EPS) * g + b


def _log_sigmoid(z):
    return jnp.minimum(z, 0.0) - jnp.log1p(jnp.exp(-jnp.abs(z)))


def _silu(x):
    return x * jax.nn.sigmoid(x)


def _gelu_exact(x):
    return 0.5 * x * (1.0 + lax.erf(x * np.float32(np.sqrt(0.5))))


def _dot(a, b, precision=None):
    return jnp.dot(a, b, preferred_element_type=F32, precision=precision)


def _dot_nt(a, b, precision=None):
    return lax.dot_general(a, b, (((1,), (1,)), ((), ())), preferred_element_type=F32, precision=precision)


def _dot_tn(a, b, precision=None):
    return lax.dot_general(a, b, (((0,), (0,)), ((), ())), preferred_element_type=F32, precision=precision)


HIGHEST = lax.Precision.HIGHEST


def _inproj_kernel(x_ref, w_ref, o_ref, xb_ref):
    @pl.when(pl.program_id(1) == 0)
    def _():
        xb_ref[...] = x_ref[...].astype(BF16)

    o_ref[...] = _dot(xb_ref[...], w_ref[...])


def _inproj(x, w):
    n, d = x.shape
    nc = w.shape[1]
    tm = min(n, 512)
    tn = nc // 3
    return pl.pallas_call(
        _inproj_kernel,
        grid=(n // tm, nc // tn),
        in_specs=[pl.BlockSpec((tm, d), lambda i, j: (i, 0)),
                  pl.BlockSpec((d, tn), lambda i, j: (0, j))],
        out_specs=pl.BlockSpec((tm, tn), lambda i, j: (i, j)),
        out_shape=jax.ShapeDtypeStruct((n, nc), F32),
        scratch_shapes=[pltpu.VMEM((tm, d), BF16)],
        compiler_params=_cparams(("parallel", "arbitrary")),
        name="inproj",
    )(x, w)


def _gla_kernel(q_ref, k_ref, v_ref, r_ref, alr_ref, w2_ref, w2t_ref, brow_ref, bcol_ref, g_ref, s0_ref,
                o_ref, sfin_ref, s_ref, oraw_ref, *, rows, pad):
    c = pl.program_id(1)
    nsteps = pl.num_programs(1)
    rr = rows + pad
    nblk = rr // GLA_SUB

    @pl.when(c == 0)
    def _():
        s_ref[...] = s0_ref[0]

    def padrows(a):
        if pad == 0:
            return a
        return jnp.concatenate([a, jnp.zeros((pad, a.shape[1]), a.dtype)], axis=0)

    q = padrows(q_ref[...]) * np.float32(GLA_DK ** -0.5)
    k = padrows(k_ref[...])
    v = padrows(v_ref[...])
    alr = padrows(alr_ref[...]).astype(BF16)

    la = _log_sigmoid(_dot(alr, w2_ref[...]) + brow_ref[...]) * np.float32(1.0 / GLA_TAU)
    lat = _log_sigmoid(_dot_nt(w2t_ref[...], alr) + bcol_ref[...]) * np.float32(1.0 / GLA_TAU)
    if pad:
        la = jnp.where(lax.broadcasted_iota(jnp.int32, la.shape, 0) < rows, la, 0.0)
        lat = jnp.where(lax.broadcasted_iota(jnp.int32, lat.shape, 1) < rows, lat, 0.0)

    ti = lax.broadcasted_iota(jnp.int32, (rr, rr), 0)
    si = lax.broadcasted_iota(jnp.int32, (rr, rr), 1)
    same = (ti // GLA_SUB) == (si // GLA_SUB)
    m_incl = jnp.where(same & (si <= ti), 1.0, 0.0).astype(F32)
    m_blk = jnp.where(same, 1.0, 0.0).astype(F32)
    b_loc = _dot(m_incl, la, HIGHEST)
    b_tot = _dot(m_blk, la, HIGHEST)
    qd = (q * jnp.exp(b_loc)).astype(BF16)
    kd = (k * jnp.exp(b_tot - b_loc)).astype(BF16)
    vb = v.astype(BF16)

    bi_r = lax.broadcasted_iota(jnp.int32, (rr, LANES), 0)
    bi_c = lax.broadcasted_iota(jnp.int32, (rr, LANES), 1)
    m_ind = jnp.where((bi_r // GLA_SUB) == bi_c, 1.0, 0.0).astype(F32)
    dcol = jnp.exp(_dot(lat, m_ind, HIGHEST))

    he_r = lax.broadcasted_iota(jnp.int32, (GLA_QK, GLA_V), 0)
    he_c = lax.broadcasted_iota(jnp.int32, (GLA_QK, GLA_V), 1)
    he = jnp.where((he_r // GLA_DK) == (he_c // GLA_DV), 1.0, 0.0).astype(BF16)
    trow = lax.broadcasted_iota(jnp.int32, (GLA_SUB, 1), 0)

    s_heads = [s_ref[h] for h in range(GLA_HEADS)]
    for i in range(nblk):
        r0 = i * GLA_SUB
        b_i = b_loc[r0:r0 + GLA_SUB]
        q_i = q[r0:r0 + GLA_SUB]
        k_i = k[r0:r0 + GLA_SUB]
        v_i = v[r0:r0 + GLA_SUB]
        parts = []
        for s in range(GLA_SUB):
            e = jnp.exp(jnp.minimum(b_i - b_i[s:s + 1], 0.0))
            parts.append(jnp.where(trow >= s, q_i * e * k_i[s:s + 1], 0.0))
        pstack = jnp.concatenate(parts, axis=0).astype(BF16)
        rexp = _dot(pstack, he)
        o_blk = rexp[0:GLA_SUB] * v_i[0:1]
        for s in range(1, GLA_SUB):
            o_blk = o_blk + rexp[s * GLA_SUB:(s + 1) * GLA_SUB] * v_i[s:s + 1]
        outs = []
        for h in range(GLA_HEADS):
            ks = slice(h * GLA_DK, (h + 1) * GLA_DK)
            vs = slice(h * GLA_DV, (h + 1) * GLA_DV)
            s_h = s_heads[h]
            outs.append(_dot(qd[r0:r0 + GLA_SUB, ks], s_h.astype(BF16)))
            upd = _dot_tn(kd[r0:r0 + GLA_SUB, ks], vb[r0:r0 + GLA_SUB, vs])
            s_heads[h] = dcol[ks, i:i + 1] * s_h + upd
        oraw_ref[r0:r0 + GLA_SUB, :] = o_blk + jnp.concatenate(outs, axis=1)
    for h in range(GLA_HEADS):
        s_ref[h] = s_heads[h]

    o = oraw_ref[0:rows, :]
    g = g_ref[...]
    normed = []
    for h in range(GLA_HEADS):
        oh = o[:, h * GLA_DV:(h + 1) * GLA_DV]
        ms = jnp.mean(oh * oh, -1, keepdims=True)
        normed.append(oh * lax.rsqrt(ms + 1e-6) * g)
    o_ref[...] = jnp.concatenate(normed, axis=1) * _silu(r_ref[...])

    @pl.when(c == nsteps - 1)
    def _():
        sfin_ref[0] = s_ref[...]


def _gla(h, s0, w2, w2t, brow, bcol, g, nseq, seqlen):
    rows = min(seqlen, 128)
    pad = (-rows) % GLA_SUB
    steps = seqlen // rows
    blk = lambda w, j: pl.BlockSpec((rows, w), lambda b, c: (b * steps + c, j))
    const = lambda shp: pl.BlockSpec(shp, lambda b, c: (0,) * len(shp))
    o, sfin = pl.pallas_call(
        functools.partial(_gla_kernel, rows=rows, pad=pad),
        grid=(nseq, steps),
        in_specs=[blk(GLA_QK, 0), blk(GLA_QK, 1), blk(GLA_V, 1), blk(GLA_V, 2), blk(LANES, COL_ALR),
                  const((LANES, GLA_QK)), const((GLA_QK, LANES)), const((1, GLA_QK)), const((GLA_QK, 1)),
                  const((1, GLA_DV)),
                  pl.BlockSpec((1, GLA_HEADS, GLA_DK, GLA_DV), lambda b, c: (b, 0, 0, 0))],
        out_specs=[pl.BlockSpec((rows, GLA_V), lambda b, c: (b * steps + c, 0)),
                   pl.BlockSpec((1, GLA_HEADS, GLA_DK, GLA_DV), lambda b, c: (b, 0, 0, 0))],
        out_shape=[jax.ShapeDtypeStruct((nseq * seqlen, GLA_V), F32),
                   jax.ShapeDtypeStruct((nseq, GLA_HEADS, GLA_DK, GLA_DV), F32)],
        scratch_shapes=[pltpu.VMEM((GLA_HEADS, GLA_DK, GLA_DV), F32),
                        pltpu.VMEM((rows + pad, GLA_V), F32)],
        compiler_params=_cparams(("parallel", "arbitrary")),
        name="gla",
    )(h, h, h, h, h, w2, w2t, brow, bcol, g, s0)
    return o, sfin


def _moba_prompt_kernel(q_ref, k_ref, v_ref, o_ref, kt_ref, vt_ref, kb_ref, vb_ref, km_ref, *, nblk):
    i = pl.program_id(2)
    blk = MOBA_BLOCK

    @pl.when(i == 0)
    def _():
        kf = k_ref[...]
        kb_ref[...] = kf.astype(BF16)
        vb_ref[...] = v_ref[...].astype(BF16)
        km_ref[...] = jnp.sum(kf.reshape(nblk, blk, LANES), axis=1) * np.float32(1.0 / blk)

    for pg in range(blk // PAGE_SIZE):
        r0 = pl.multiple_of(i * blk + pg * PAGE_SIZE, PAGE_SIZE)
        kt_ref[0, pg] = k_ref[pl.ds(r0, PAGE_SIZE), :].T.reshape(2, MOBA_DH, PAGE_SIZE)
        vt_ref[0, pg] = v_ref[pl.ds(r0, PAGE_SIZE), :].T.reshape(2, MOBA_DH, PAGE_SIZE)

    qs = q_ref[...] * np.float32(MOBA_DH ** -0.5)
    lane = lax.broadcasted_iota(jnp.int32, (blk, LANES), 1)
    col = lax.broadcasted_iota(jnp.int32, (blk, nblk), 1)
    km = km_ref[...]
    row2 = lax.broadcasted_iota(jnp.int32, (blk, blk), 0)
    col2 = lax.broadcasted_iota(jnp.int32, (blk, blk), 1)
    own0 = pl.multiple_of(i * blk, blk)
    k_own = kb_ref[pl.ds(own0, blk), :]
    v_own = vb_ref[pl.ds(own0, blk), :]

    qms, sels, carry = [], [], []
    for e in range(2):
        head_lanes = (lane < MOBA_DH) if e == 0 else (lane >= MOBA_DH)
        qm = jnp.where(head_lanes, qs, 0.0)
        gate = _dot_nt(qm, km, HIGHEST)
        gsel = jnp.where(col < i, gate, NEG_INF)
        sel = jnp.zeros((blk, nblk), F32)
        for _ in range(MOBA_TOPK):
            m = jnp.max(gsel, -1, keepdims=True)
            idx = jnp.min(jnp.where(gsel == m, col, nblk), -1, keepdims=True)
            hit = col == idx
            sel = jnp.where(hit & (m > NEG_INF), 1.0, sel)
            gsel = jnp.where(hit, NEG_INF, gsel)
        qmb = qm.astype(BF16)
        s = jnp.where(col2 <= row2, _dot_nt(qmb, k_own), NEG_INF)
        m0 = jnp.max(s, -1, keepdims=True)
        p = jnp.exp(s - m0)
        carry += [m0, jnp.sum(p, -1, keepdims=True), _dot(p.astype(BF16), v_own)]
        qms.append(qmb)
        sels.append(sel)

    def body(j, c):
        j0 = pl.multiple_of(j * blk, blk)
        kj = kb_ref[pl.ds(j0, blk), :]
        vj = vb_ref[pl.ds(j0, blk), :]
        out = []
        for e in range(2):
            m_run, l_run, acc = c[3 * e:3 * e + 3]
            selj = jnp.max(jnp.where(col == j, sels[e], 0.0), -1, keepdims=True)
            s = jnp.where(selj > 0.0, _dot_nt(qms[e], kj), NEG_INF)
            m_new = jnp.maximum(m_run, jnp.max(s, -1, keepdims=True))
            a = jnp.exp(m_run - m_new)
            p = jnp.exp(s - m_new)
            out += [m_new, a * l_run + jnp.sum(p, -1, keepdims=True), a * acc + _dot(p.astype(BF16), vj)]
        return tuple(out)

    res = lax.fori_loop(0, i, body, tuple(carry))
    o0 = res[2] / res[1]
    o1 = res[5] / res[4]
    o_ref[...] = jnp.where(lane < MOBA_DH, o0, o1)


def _moba_prompt(h, nseq, seqlen):
    nblk = seqlen // MOBA_BLOCK
    npair = MOBA_HEADS // 2
    ppb = MOBA_BLOCK // PAGE_SIZE
    o, kt, vt = pl.pallas_call(
        functools.partial(_moba_prompt_kernel, nblk=nblk),
        grid=(nseq, npair, nblk),
        in_specs=[pl.BlockSpec((MOBA_BLOCK, LANES), lambda b, p, i: (b * nblk + i, COL_QB + p)),
                  pl.BlockSpec((seqlen, LANES), lambda b, p, i: (b, COL_KB + p)),
                  pl.BlockSpec((seqlen, LANES), lambda b, p, i: (b, COL_VB + p))],
        out_specs=[pl.BlockSpec((MOBA_BLOCK, LANES), lambda b, p, i: (b * nblk + i, p)),
                   pl.BlockSpec((1, ppb, 2, MOBA_DH, PAGE_SIZE), lambda b, p, i: (b, i, p, 0, 0)),
                   pl.BlockSpec((1, ppb, 2, MOBA_DH, PAGE_SIZE), lambda b, p, i: (b, i, p, 0, 0))],
        out_shape=[jax.ShapeDtypeStruct((nseq * seqlen, MOBA_W), F32),
                   jax.ShapeDtypeStruct((nseq, seqlen // PAGE_SIZE, MOBA_HEADS, MOBA_DH, PAGE_SIZE), F32),
                   jax.ShapeDtypeStruct((nseq, seqlen // PAGE_SIZE, MOBA_HEADS, MOBA_DH, PAGE_SIZE), F32)],
        scratch_shapes=[pltpu.VMEM((seqlen, LANES), BF16), pltpu.VMEM((seqlen, LANES), BF16),
                        pltpu.VMEM((nblk, LANES), F32)],
        compiler_params=_cparams(("parallel", "parallel", "arbitrary")),
        name="moba_prompt",
    )(h, h, h)
    return o, kt, vt


MOBA_NBUF = 4


def _moba_sample_kernel(pt_ref, q_ref, kn_ref, vn_ref, kc_ref, vc_ref, o_ref,
                        buf_ref, sem_ref, sall_ref, ksum_ref, *, layer, npages, tq):
    b = pl.program_id(0)
    hd = MOBA_HEADS * MOBA_DH
    nrow = MOBA_HEADS * tq
    ppb = MOBA_BLOCK // PAGE_SIZE
    nfull = npages // ppb
    nbuf = MOBA_NBUF
    total = 2 * npages

    def page_copy(n, slot):
        is_k = n < npages
        pg = jnp.where(is_k, n, n - npages)
        phys = pt_ref[b, pg]
        return is_k, (pltpu.make_async_copy(kc_ref.at[layer, phys], buf_ref.at[slot], sem_ref.at[slot]),
                      pltpu.make_async_copy(vc_ref.at[layer, phys], buf_ref.at[slot], sem_ref.at[slot]))

    def start(n):
        slot = n % nbuf
        is_k, (ck, cv) = page_copy(n, slot)

        @pl.when(is_k)
        def _():
            ck.start()

        @pl.when(jnp.logical_not(is_k))
        def _():
            cv.start()

    def wait(n):
        slot = n % nbuf
        _, (ck, _) = page_copy(n, slot)
        ck.wait()

    for n in range(nbuf - 1):
        start(jnp.int32(n))

    qs = q_ref[...] * np.float32(MOBA_DH ** -0.5)
    qt = jnp.concatenate([qs] * MOBA_HEADS, axis=0)
    r_i = lax.broadcasted_iota(jnp.int32, (nrow, hd), 0)
    c_i = lax.broadcasted_iota(jnp.int32, (nrow, hd), 1)
    diag = (r_i // tq) == (c_i // MOBA_DH)
    qbd = jnp.where(diag, qt, 0.0)
    qbd_b = qbd.astype(BF16)
    lane = lax.broadcasted_iota(jnp.int32, (hd, LANES), 1)
    ksum_ref[...] = jnp.zeros((hd, LANES), F32)

    def k_body(j, carry):
        nxt = j + nbuf - 1

        @pl.when(nxt < total)
        def _():
            start(nxt)

        wait(j)
        kp = buf_ref[j % nbuf].reshape(hd, PAGE_SIZE)
        sall_ref[j] = _dot(qbd_b, kp.astype(BF16))
        rs = jnp.sum(kp, -1, keepdims=True)
        ksum_ref[...] += jnp.where(lane == j // ppb, rs, 0.0)
        return carry

    lax.fori_loop(0, npages, k_body, 0)

    gate = _dot(qbd, ksum_ref[...], HIGHEST) * np.float32(1.0 / MOBA_BLOCK)
    gl = lax.broadcasted_iota(jnp.int32, (nrow, LANES), 1)
    gsel = jnp.where(gl < nfull, gate, NEG_INF)
    sel = jnp.zeros((nrow, LANES), F32)
    for _ in range(min(MOBA_TOPK, nfull)):
        m = jnp.max(gsel, -1, keepdims=True)
        idx = jnp.min(jnp.where(gsel == m, gl, LANES), -1, keepdims=True)
        hit = gl == idx
        sel = jnp.where(hit & (m > NEG_INF), 1.0, sel)
        gsel = jnp.where(hit, NEG_INF, gsel)

    kn = jnp.concatenate([kn_ref[...], jnp.zeros((LANES - tq, hd), F32)], axis=0).astype(BF16)
    vn = jnp.concatenate([vn_ref[...], jnp.zeros((LANES - tq, hd), F32)], axis=0).astype(BF16)
    s_new = _dot_nt(qbd_b, kn)
    nr = lax.broadcasted_iota(jnp.int32, (nrow, LANES), 0)
    s_new = jnp.where(gl <= (nr % tq), s_new, NEG_INF)

    def sel_col(j):
        return jnp.max(jnp.where(gl == j // ppb, sel, 0.0), -1, keepdims=True) > 0.0

    def max_body(j, m_run):
        s = jnp.where(sel_col(j), sall_ref[j], NEG_INF)
        return jnp.maximum(m_run, jnp.max(s, -1, keepdims=True))

    m_all = lax.fori_loop(0, npages, max_body, jnp.max(s_new, -1, keepdims=True))
    p_new = jnp.exp(s_new - m_all)
    l0 = jnp.sum(p_new, -1, keepdims=True)
    acc0 = _dot(p_new.astype(BF16), vn)

    def v_body(j, carry):
        l_run, acc = carry
        n = npages + j
        nxt = n + nbuf - 1

        @pl.when(nxt < total)
        def _():
            start(nxt)

        wait(n)
        vp = buf_ref[n % nbuf].reshape(hd, PAGE_SIZE).astype(BF16)
        p = jnp.exp(jnp.where(sel_col(j), sall_ref[j], NEG_INF) - m_all)
        return l_run + jnp.sum(p, -1, keepdims=True), acc + _dot_nt(p.astype(BF16), vp)

    l_all, acc = lax.fori_loop(0, npages, v_body, (l0, acc0))
    res = jnp.where(diag, acc / l_all, 0.0).reshape(MOBA_HEADS, tq, hd)
    o_ref[...] = jnp.sum(res, axis=0)


def _moba_sample(h, page_table, cache_kt, cache_vt, layer, nseq, tq):
    npages = page_table.shape[1]
    hd = MOBA_W
    grid_spec = pltpu.PrefetchScalarGridSpec(
        num_scalar_prefetch=1,
        grid=(nseq,),
        in_specs=[pl.BlockSpec((tq, hd), lambda b, pt: (b, COL_QB // 4)),
                  pl.BlockSpec((tq, hd), lambda b, pt: (b, COL_KB // 4)),
                  pl.BlockSpec((tq, hd), lambda b, pt: (b, COL_VB // 4)),
                  pl.BlockSpec(memory_space=pl.ANY),
                  pl.BlockSpec(memory_space=pl.ANY)],
        out_specs=pl.BlockSpec((tq, hd), lambda b, pt: (b, 0)),
        scratch_shapes=[pltpu.VMEM((MOBA_NBUF, MOBA_HEADS, MOBA_DH, PAGE_SIZE), F32),
                        pltpu.SemaphoreType.DMA((MOBA_NBUF,)),
                        pltpu.VMEM((npages, MOBA_HEADS * tq, PAGE_SIZE), F32),
                        pltpu.VMEM((hd, LANES), F32)],
    )
    return pl.pallas_call(
        functools.partial(_moba_sample_kernel, layer=layer, npages=npages, tq=tq),
        grid_spec=grid_spec,
        out_shape=jax.ShapeDtypeStruct((nseq * tq, hd), F32),
        compiler_params=_cparams(("arbitrary",)),
        name="moba_sample",
    )(page_table, h, h, h, cache_kt, cache_vt)


def _sgu_kernel(u_ref, v_ref, g_ref, b_ref, w_ref, bias_ref, o_ref, *maybe_v_out, period):
    rows = u_ref.shape[0]
    u = _gelu_exact(u_ref[...])
    vn = _layer_norm(_gelu_exact(v_ref[...]), g_ref[...], b_ref[...])
    if maybe_v_out:
        maybe_v_out[0][...] = vn
    t = lax.broadcasted_iota(jnp.int32, (rows, rows), 0)
    s = lax.broadcasted_iota(jnp.int32, (rows, rows), 1)
    keep = ((t // period) == (s // period)) & ((s % period) <= (t % period))
    vb = vn.astype(BF16)
    mix = []
    for g in range(SG_GROUPS):
        wg = jnp.where(keep, w_ref[g], 0.0).astype(BF16)
        mix.append(_dot(wg, vb[:, g * SG_GC:(g + 1) * SG_GC]))
    o_ref[...] = u * (jnp.concatenate(mix, axis=1) + bias_ref[...])


def _sgu(h, ln_g, ln_b, wmix, bias, rows, period, want_v):
    n = h.shape[0]
    const = lambda shp: pl.BlockSpec(shp, lambda i: (0,) * len(shp))
    out_specs = [pl.BlockSpec((rows, SG_W), lambda i: (i, 0))]
    out_shape = [jax.ShapeDtypeStruct((n, SG_W), F32)]
    if want_v:
        out_specs.append(pl.BlockSpec((rows, SG_W), lambda i: (i, 0)))
        out_shape.append(jax.ShapeDtypeStruct((n, SG_W), F32))
    res = pl.pallas_call(
        functools.partial(_sgu_kernel, period=period),
        grid=(n // rows,),
        in_specs=[pl.BlockSpec((rows, SG_W), lambda i: (i, 6)),
                  pl.BlockSpec((rows, SG_W), lambda i: (i, 7)),
                  const((1, SG_W)), const((1, SG_W)),
                  const((SG_GROUPS, rows, rows)), const((rows, SG_W))],
        out_specs=out_specs,
        out_shape=out_shape,
        compiler_params=_cparams(("parallel",)),
        name="sgu",
    )(h, h, ln_g, ln_b, wmix, bias)
    return res if want_v else (res[0], None)


def _merge_kernel(oa_ref, ob_ref, oc_ref, ga_ref, gb_ref, gc_ref, x_ref, pa_ref, pb_ref, pc_ref, wo_ref,
                  g_ref, b_ref, o_ref):
    m = (jax.nn.sigmoid(ga_ref[...]) * _dot(oa_ref[...].astype(BF16), pa_ref[...])
         + jax.nn.sigmoid(gb_ref[...]) * _dot(ob_ref[...].astype(BF16), pb_ref[...])
         + jax.nn.sigmoid(gc_ref[...]) * _dot(oc_ref[...].astype(BF16), pc_ref[...]))
    y = np.float32(ALPHA) * x_ref[...] + _dot(m.astype(BF16), wo_ref[...])
    o_ref[...] = _layer_norm(y, g_ref[...], b_ref[...])


def _merge(oa, ob, oc, h, x, pa, pb, pc, wo, g, b):
    n = x.shape[0]
    tm = min(n, 512)
    row = lambda w, j: pl.BlockSpec((tm, w), lambda i: (i, j))
    const = lambda shp: pl.BlockSpec(shp, lambda i: (0,) * len(shp))
    return pl.pallas_call(
        _merge_kernel,
        grid=(n // tm,),
        in_specs=[row(GLA_V, 0), row(MOBA_W, 0), row(SG_W, 0),
                  row(D_MODEL, 4), row(D_MODEL, 5), row(D_MODEL, 6), row(D_MODEL, 0),
                  const((GLA_V, D_MODEL)), const((MOBA_W, D_MODEL)), const((SG_W, D_MODEL)),
                  const((D_MODEL, D_MODEL)), const((1, D_MODEL)), const((1, D_MODEL))],
        out_specs=row(D_MODEL, 0),
        out_shape=jax.ShapeDtypeStruct((n, D_MODEL), F32),
        compiler_params=_cparams(("parallel",)),
        name="merge",
    )(oa, ob, oc, h, h, h, x, pa, pb, pc, wo, g, b)


MOE_CHUNK_EXPERTS = 8


def _moe_kernel(x_ref, wr_ref, br_ref, wg_ref, wu_ref, wd_ref, sg_ref, su_ref, sd_ref, g_ref, b_ref,
                o_ref, xb_ref, gates_ref, acc_ref):
    c = pl.program_id(1)
    nchunk = pl.num_programs(1)
    tm = x_ref.shape[0]
    ce = MOE_CHUNK_EXPERTS
    gsz = N_EXPERTS // N_EXPERT_GROUPS

    @pl.when(c == 0)
    def _():
        x = x_ref[...]
        xb = x.astype(BF16)
        xb_ref[...] = xb
        s = jax.nn.sigmoid(_dot_nt(wr_ref[...], x, HIGHEST))
        sel = s + br_ref[...]
        sel3 = sel.reshape(N_EXPERT_GROUPS, gsz, tm)
        j3 = lax.broadcasted_iota(jnp.int32, sel3.shape, 1)
        m1 = jnp.max(sel3, axis=1, keepdims=True)
        i1 = jnp.min(jnp.where(sel3 == m1, j3, gsz), axis=1, keepdims=True)
        m2 = jnp.max(jnp.where(j3 == i1, NEG_INF, sel3), axis=1, keepdims=True)
        grp = (m1 + m2).reshape(N_EXPERT_GROUPS, tm)
        gi = lax.broadcasted_iota(jnp.int32, grp.shape, 0)
        gkeep = jnp.zeros(grp.shape, F32)
        for _ in range(TOPK_GROUPS):
            m = jnp.max(grp, axis=0, keepdims=True)
            idx = jnp.min(jnp.where(grp == m, gi, N_EXPERT_GROUPS), axis=0, keepdims=True)
            hit = gi == idx
            gkeep = jnp.where(hit, 1.0, gkeep)
            grp = jnp.where(hit, NEG_INF, grp)
        emask = jnp.broadcast_to(gkeep.reshape(N_EXPERT_GROUPS, 1, tm), sel3.shape).reshape(N_EXPERTS, tm)
        cand = jnp.where(emask > 0.0, sel, NEG_INF)
        ei = lax.broadcasted_iota(jnp.int32, cand.shape, 0)
        chosen = jnp.zeros(cand.shape, F32)
        for _ in range(TOP_K):
            m = jnp.max(cand, axis=0, keepdims=True)
            idx = jnp.min(jnp.where(cand == m, ei, N_EXPERTS), axis=0, keepdims=True)
            hit = ei == idx
            chosen = jnp.where(hit, 1.0, chosen)
            cand = jnp.where(hit, NEG_INF, cand)
        w = jnp.where(chosen > 0.0, s, 0.0)
        gates_ref[...] = (w / jnp.sum(w, axis=0, keepdims=True) * np.float32(ROUTED_SCALE)).reshape(
            N_EXPERTS // ce, ce, tm)
        hs = _silu(_dot(xb, sg_ref[...])) * _dot(xb, su_ref[...])
        acc_ref[...] = _dot(hs.astype(BF16), sd_ref[...])

    xb = xb_ref[...]
    hh = _silu(_dot(xb, wg_ref[...])) * _dot(xb, wu_ref[...])
    gt = jnp.concatenate([gates_ref[c], jnp.zeros((LANES - ce, tm), F32)], axis=0).T
    gexp = jnp.concatenate([jnp.broadcast_to(gt[:, j:j + 1], (tm, D_EXPERT)) for j in range(ce)], axis=1)
    acc_ref[...] += _dot((hh * gexp).astype(BF16), wd_ref[...])

    @pl.when(c == nchunk - 1)
    def _():
        o_ref[...] = _layer_norm(np.float32(ALPHA) * x_ref[...] + acc_ref[...], g_ref[...], b_ref[...])


def _moe(x, wr_t, br, wg, wu, wd, sg, su, sd, g, b):
    n = x.shape[0]
    tm = min(n, 512)
    cw = MOE_CHUNK_EXPERTS * D_EXPERT
    nchunk = N_EXPERTS // MOE_CHUNK_EXPERTS
    const = lambda shp: pl.BlockSpec(shp, lambda i, c: (0,) * len(shp))
    return pl.pallas_call(
        _moe_kernel,
        grid=(n // tm, nchunk),
        in_specs=[pl.BlockSpec((tm, D_MODEL), lambda i, c: (i, 0)),
                  const((N_EXPERTS, D_MODEL)), const((N_EXPERTS, 1)),
                  pl.BlockSpec((D_MODEL, cw), lambda i, c: (0, c)),
                  pl.BlockSpec((D_MODEL, cw), lambda i, c: (0, c)),
                  pl.BlockSpec((cw, D_MODEL), lambda i, c: (c, 0)),
                  const((D_MODEL, D_SHARED)), const((D_MODEL, D_SHARED)), const((D_SHARED, D_MODEL)),
                  const((1, D_MODEL)), const((1, D_MODEL))],
        out_specs=pl.BlockSpec((tm, D_MODEL), lambda i, c: (i, 0)),
        out_shape=jax.ShapeDtypeStruct((n, D_MODEL), F32),
        scratch_shapes=[pltpu.VMEM((tm, D_MODEL), BF16),
                        pltpu.VMEM((nchunk, MOE_CHUNK_EXPERTS, tm), F32),
                        pltpu.VMEM((tm, D_MODEL), F32)],
        compiler_params=_cparams(("parallel", "arbitrary")),
        name="moe",
    )(x, wr_t, br, wg, wu, wd, sg, su, sd, g, b)


def _prep_weights(w_in, gla_w_a2, gla_b_a2, gla_norm_g, sg_ln_g, sg_ln_b, sg_w, sg_b, w_branch_a, w_branch_b,
                  w_branch_c, w_out, ln1_g, ln1_b, ln2_g, ln2_b, moe_w_router, moe_b_router, moe_w_gate,
                  moe_w_up, moe_w_down, sh_w_gate, sh_w_up, sh_w_down, dec_seq):
    depth = w_in.shape[0]
    c_alr = 2 * GLA_QK + 2 * GLA_V
    w_perm = jnp.concatenate(
        [w_in[:, :, :c_alr], w_in[:, :, c_alr + GLA_RANK:], w_in[:, :, c_alr:c_alr + GLA_RANK],
         jnp.zeros((depth, D_MODEL, LANES - GLA_RANK), w_in.dtype)], axis=2).astype(BF16)
    w2 = jnp.concatenate([gla_w_a2, jnp.zeros((depth, LANES - GLA_RANK, GLA_QK), F32)], axis=1).astype(BF16)
    reps = PAGE_SIZE // dec_seq if dec_seq < SG_CHUNK else 1
    nrow_s = 256
    tile_s = nrow_s // dec_seq
    return dict(
        w_perm=w_perm,
        w2=w2, w2t=jnp.swapaxes(w2, 1, 2),
        brow=gla_b_a2[:, None, :], bcol=gla_b_a2[:, :, None],
        gla_g=gla_norm_g[:, None, :],
        sg_g=sg_ln_g[:, None, :], sg_b=sg_ln_b[:, None, :],
        sg_w_p=sg_w,
        sg_bias_p=jnp.repeat(jnp.swapaxes(sg_b, 1, 2), SG_GC, axis=2),
        sg_w_s=jnp.tile(sg_w[:, :, :dec_seq, :dec_seq], (1, 1, tile_s, tile_s)),
        sg_bias_s=jnp.tile(jnp.repeat(jnp.swapaxes(sg_b[:, :, :dec_seq], 1, 2), SG_GC, axis=2), (1, tile_s, 1)),
        pa=w_branch_a.astype(BF16), pb=w_branch_b.astype(BF16), pc=w_branch_c.astype(BF16),
        wo=w_out.astype(BF16),
        ln1_g=ln1_g[:, None, :], ln1_b=ln1_b[:, None, :], ln2_g=ln2_g[:, None, :], ln2_b=ln2_b[:, None, :],
        wr_t=jnp.swapaxes(moe_w_router, 1, 2), br=moe_b_router[:, :, None],
        wg=jnp.transpose(moe_w_gate, (0, 2, 1, 3)).reshape(depth, D_MODEL, N_EXPERTS * D_EXPERT).astype(BF16),
        wu=jnp.transpose(moe_w_up, (0, 2, 1, 3)).reshape(depth, D_MODEL, N_EXPERTS * D_EXPERT).astype(BF16),
        wd=moe_w_down.reshape(depth, N_EXPERTS * D_EXPERT, D_MODEL).astype(BF16),
        sg=sh_w_gate.astype(BF16), su=sh_w_up.astype(BF16), sd=sh_w_down.astype(BF16),
    )


def _token_tail(x, h, oa, ob, oc, w, l):
    x1 = _merge(oa, ob, oc, h, x, w["pa"][l], w["pb"][l], w["pc"][l], w["wo"][l], w["ln1_g"][l], w["ln1_b"][l])
    return _moe(x1, w["wr_t"][l], w["br"][l], w["wg"][l], w["wu"][l], w["wd"][l], w["sg"][l], w["su"][l],
                w["sd"][l], w["ln2_g"][l], w["ln2_b"][l])


def kernel(x_prompt, x_sample, cache_k, cache_v, state_gla, page_table, w_in, gla_w_a2, gla_b_a2, gla_norm_g,
           sg_ln_g, sg_ln_b, sg_w, sg_b, w_branch_a, w_branch_b, w_branch_c, w_out, ln1_g, ln1_b, ln2_g, ln2_b,
           moe_w_router, moe_b_router, moe_w_gate, moe_w_up, moe_w_down, sh_w_gate, sh_w_up, sh_w_down):
    bp, sp, d = x_prompt.shape
    db, t, _ = x_sample.shape
    depth = w_in.shape[0]
    assert d == D_MODEL and sp % MOBA_BLOCK == 0 and sp % SG_CHUNK == 0 and t <= SUBLANES
    assert (db * t) % 256 == 0 or db * t == 256
    w = _prep_weights(w_in, gla_w_a2, gla_b_a2, gla_norm_g, sg_ln_g, sg_ln_b, sg_w, sg_b, w_branch_a,
                      w_branch_b, w_branch_c, w_out, ln1_g, ln1_b, ln2_g, ln2_b, moe_w_router, moe_b_router,
                      moe_w_gate, moe_w_up, moe_w_down, sh_w_gate, sh_w_up, sh_w_down, t)
    cache_kt = jnp.swapaxes(cache_k, 3, 4)
    cache_vt = jnp.swapaxes(cache_v, 3, 4)
    xp = x_prompt.reshape(bp * sp, d)
    xs = x_sample.reshape(db * t, d)
    zero_state = jnp.zeros((bp, GLA_HEADS, GLA_DK, GLA_DV), F32)
    gla_p, gla_s, kp_l, vp_l, ks_l, vs_l, sgu_l = [], [], [], [], [], [], []
    for l in range(depth):
        gla_args = (w["w2"][l], w["w2t"][l], w["brow"][l], w["bcol"][l], w["gla_g"][l])
        hp = _inproj(xp, w["w_perm"][l])
        oa, s_p = _gla(hp, zero_state, *gla_args, bp, sp)
        ob, kt, vt = _moba_prompt(hp, bp, sp)
        oc, _ = _sgu(hp, w["sg_g"][l], w["sg_b"][l], w["sg_w_p"][l], w["sg_bias_p"][l], SG_CHUNK, SG_CHUNK, False)
        xp = _token_tail(xp, hp, oa, ob, oc, w, l)
        hs = _inproj(xs, w["w_perm"][l])
        oa, s_s = _gla(hs, state_gla[l], *gla_args, db, t)
        ob = _moba_sample(hs, page_table, cache_kt, cache_vt, l, db, t)
        oc, vcs = _sgu(hs, w["sg_g"][l], w["sg_b"][l], w["sg_w_s"][l], w["sg_bias_s"][l], db * t, t, True)
        xs = _token_tail(xs, hs, oa, ob, oc, w, l)
        gla_p.append(s_p)
        gla_s.append(s_s)
        kp_l.append(jnp.swapaxes(kt, 3, 4))
        vp_l.append(jnp.swapaxes(vt, 3, 4))
        kb_s = hs[:, COL_KB * LANES:COL_KB * LANES + MOBA_W].reshape(db, t, MOBA_HEADS, MOBA_DH)
        vb_s = hs[:, COL_VB * LANES:COL_VB * LANES + MOBA_W].reshape(db, t, MOBA_HEADS, MOBA_DH)
        ks_l.append(kb_s.transpose(0, 2, 1, 3))
        vs_l.append(vb_s.transpose(0, 2, 1, 3))
        sgu_l.append(vcs.reshape(db, t, SG_W))
    return (xp.reshape(bp, sp, d), xs.reshape(db, t, d), jnp.stack(gla_p), jnp.stack(gla_s),
            jnp.stack(kp_l), jnp.stack(vp_l), jnp.stack(ks_l), jnp.stack(vs_l), jnp.stack(sgu_l))
```

```python
import functools
import math

import numpy as np
import jax
import jax.numpy as jnp
from jax import lax
from jax.experimental import pallas as pl
from jax.experimental.pallas import tpu as pltpu

F32 = jnp.float32
BF16 = jnp.bfloat16

D_MODEL = 1024
DEPTH = 4
PAGE_SIZE = 128
GLA_HEADS, GLA_DK, GLA_DV, GLA_RANK, GLA_TAU = 4, 64, 128, 16, 16.0
GLA_QK, GLA_V = GLA_HEADS * GLA_DK, GLA_HEADS * GLA_DV
MOBA_HEADS, MOBA_DH, MOBA_BLOCK, MOBA_TOPK = 8, 64, 256, 3
MOBA_W = MOBA_HEADS * MOBA_DH
SG_GROUPS, SG_GC, SG_CHUNK = 4, 128, 128
SG_W = SG_GROUPS * SG_GC
N_EXPERTS, TOP_K, N_EXPERT_GROUPS, TOPK_GROUPS = 64, 8, 8, 4
D_EXPERT, D_SHARED, ROUTED_SCALE = 128, 128, 2.5
ALPHA = (2 * DEPTH) ** 0.25
LN_EPS = 1e-5

LANES = 128
SUBLANES = 8
VMEM_LIMIT_BYTES = 56 * 1024 * 1024

N_IN_PERM = 2 * GLA_QK + 2 * GLA_V + 3 * MOBA_W + 2 * SG_W + 3 * D_MODEL + LANES
COL_QB, COL_KB, COL_VB = 12, 16, 20
COL_ALR = 56
GLA_SUB = 16
NEG_INF = float("-inf")


def _cparams(sem):
    return pltpu.CompilerParams(dimension_semantics=sem, vmem_limit_bytes=VMEM_LIMIT_BYTES)


def _layer_norm(x, g, b):
    mu = jnp.mean(x, -1, keepdims=True)
    xc = x - mu
    var = jnp.mean(xc * xc, -1, keepdims=True)
    return xc * lax.rsqrt(var + LN_EPS) * g + b


def _log_sigmoid(z):
    return jnp.minimum(z, 0.0) - jnp.log1p(jnp.exp(-jnp.abs(z)))


def _silu(x):
    return x * jax.nn.sigmoid(x)


def _gelu_exact(x):
    return 0.5 * x * (1.0 + lax.erf(x * np.float32(np.sqrt(0.5))))


def _dot(a, b, precision=None):
    return jnp.dot(a, b, preferred_element_type=F32, precision=precision)


def _dot_nt(a, b, precision=None):
    return lax.dot_general(a, b, (((1,), (1,)), ((), ())), preferred_element_type=F32, precision=precision)


def _dot_tn(a, b, precision=None):
    return lax.dot_general(a, b, (((0,), (0,)), ((), ())), preferred_element_type=F32, precision=precision)


HIGHEST = lax.Precision.HIGHEST


def _inproj_kernel(x_ref, w_ref, o_ref, xb_ref):
    @pl.when(pl.program_id(1) == 0)
    def _():
        xb_ref[...] = x_ref[...].astype(BF16)

    o_ref[...] = _dot(xb_ref[...], w_ref[...])


def _inproj(x, w):
    n, d = x.shape
    nc = w.shape[1]
    tm = min(n, 512)
    tn = nc // 3
    return pl.pallas_call(
        _inproj_kernel,
        grid=(n // tm, nc // tn),
        in_specs=[pl.BlockSpec((tm, d), lambda i, j: (i, 0)),
                  pl.BlockSpec((d, tn), lambda i, j: (0, j))],
        out_specs=pl.BlockSpec((tm, tn), lambda i, j: (i, j)),
        out_shape=jax.ShapeDtypeStruct((n, nc), F32),
        scratch_shapes=[pltpu.VMEM((tm, d), BF16)],
        compiler_params=_cparams(("parallel", "arbitrary")),
        name="inproj",
    )(x, w)


def _gla_kernel(q_ref, k_ref, v_ref, r_ref, alr_ref, w2_ref, w2t_ref, brow_ref, bcol_ref, g_ref, s0_ref,
                o_ref, sfin_ref, s_ref, oraw_ref, *, rows, pad):
    c = pl.program_id(1)
    nsteps = pl.num_programs(1)
    rr = rows + pad
    nblk = rr // GLA_SUB

    @pl.when(c == 0)
    def _():
        s_ref[...] = s0_ref[0]

    def padrows(a):
        if pad == 0:
            return a
        return jnp.concatenate([a, jnp.zeros((pad, a.shape[1]), a.dtype)], axis=0)

    q = padrows(q_ref[...]) * np.float32(GLA_DK ** -0.5)
    k = padrows(k_ref[...])
    v = padrows(v_ref[...])
    alr = padrows(alr_ref[...]).astype(BF16)

    la = _log_sigmoid(_dot(alr, w2_ref[...]) + brow_ref[...]) * np.float32(1.0 / GLA_TAU)
    lat = _log_sigmoid(_dot_nt(w2t_ref[...], alr) + bcol_ref[...]) * np.float32(1.0 / GLA_TAU)
    if pad:
        la = jnp.where(lax.broadcasted_iota(jnp.int32, la.shape, 0) < rows, la, 0.0)
        lat = jnp.where(lax.broadcasted_iota(jnp.int32, lat.shape, 1) < rows, lat, 0.0)

    ti = lax.broadcasted_iota(jnp.int32, (rr, rr), 0)
    si = lax.broadcasted_iota(jnp.int32, (rr, rr), 1)
    same = (ti // GLA_SUB) == (si // GLA_SUB)
    m_incl = jnp.where(same & (si <= ti), 1.0, 0.0).astype(F32)
    m_blk = jnp.where(same, 1.0, 0.0).astype(F32)
    b_loc = _dot(m_incl, la, HIGHEST)
    b_tot = _dot(m_blk, la, HIGHEST)
    qd = (q * jnp.exp(b_loc)).astype(BF16)
    kd = (k * jnp.exp(b_tot - b_loc)).astype(BF16)
    vb = v.astype(BF16)

    bi_r = lax.broadcasted_iota(jnp.int32, (rr, LANES), 0)
    bi_c = lax.broadcasted_iota(jnp.int32, (rr, LANES), 1)
    m_ind = jnp.where((bi_r // GLA_SUB) == bi_c, 1.0, 0.0).astype(F32)
    dcol = jnp.exp(_dot(lat, m_ind, HIGHEST))

    he_r = lax.broadcasted_iota(jnp.int32, (GLA_QK, GLA_V), 0)
    he_c = lax.broadcasted_iota(jnp.int32, (GLA_QK, GLA_V), 1)
    he = jnp.where((he_r // GLA_DK) == (he_c // GLA_DV), 1.0, 0.0).astype(BF16)
    trow = lax.broadcasted_iota(jnp.int32, (GLA_SUB, 1), 0)

    s_heads = [s_ref[h] for h in range(GLA_HEADS)]
    for i in range(nblk):
        r0 = i * GLA_SUB
        b_i = b_loc[r0:r0 + GLA_SUB]
        q_i = q[r0:r0 + GLA_SUB]
        k_i = k[r0:r0 + GLA_SUB]
        v_i = v[r0:r0 + GLA_SUB]
        parts = []
        for s in range(GLA_SUB):
            e = jnp.exp(jnp.minimum(b_i - b_i[s:s + 1], 0.0))
            parts.append(jnp.where(trow >= s, q_i * e * k_i[s:s + 1], 0.0))
        pstack = jnp.concatenate(parts, axis=0).astype(BF16)
        rexp = _dot(pstack, he)
        o_blk = rexp[0:GLA_SUB] * v_i[0:1]
        for s in range(1, GLA_SUB):
            o_blk = o_blk + rexp[s * GLA_SUB:(s + 1) * GLA_SUB] * v_i[s:s + 1]
        outs = []
        for h in range(GLA_HEADS):
            ks = slice(h * GLA_DK, (h + 1) * GLA_DK)
            vs = slice(h * GLA_DV, (h + 1) * GLA_DV)
            s_h = s_heads[h]
            outs.append(_dot(qd[r0:r0 + GLA_SUB, ks], s_h.astype(BF16)))
            upd = _dot_tn(kd[r0:r0 + GLA_SUB, ks], vb[r0:r0 + GLA_SUB, vs])
            s_heads[h] = dcol[ks, i:i + 1] * s_h + upd
        oraw_ref[r0:r0 + GLA_SUB, :] = o_blk + jnp.concatenate(outs, axis=1)
    for h in range(GLA_HEADS):
        s_ref[h] = s_heads[h]

    o = oraw_ref[0:rows, :]
    g = g_ref[...]
    normed = []
    for h in range(GLA_HEADS):
        oh = o[:, h * GLA_DV:(h + 1) * GLA_DV]
        ms = jnp.mean(oh * oh, -1, keepdims=True)
        normed.append(oh * lax.rsqrt(ms + 1e-6) * g)
    o_ref[...] = jnp.concatenate(normed, axis=1) * _silu(r_ref[...])

    @pl.when(c == nsteps - 1)
    def _():
        sfin_ref[0] = s_ref[...]


def _gla(h, s0, w2, w2t, brow, bcol, g, nseq, seqlen):
    rows = min(seqlen, 128)
    pad = (-rows) % GLA_SUB
    steps = seqlen // rows
    blk = lambda w, j: pl.BlockSpec((rows, w), lambda b, c: (b * steps + c, j))
    const = lambda shp: pl.BlockSpec(shp, lambda b, c: (0,) * len(shp))
    o, sfin = pl.pallas_call(
        functools.partial(_gla_kernel, rows=rows, pad=pad),
        grid=(nseq, steps),
        in_specs=[blk(GLA_QK, 0), blk(GLA_QK, 1), blk(GLA_V, 1), blk(GLA_V, 2), blk(LANES, COL_ALR),
                  const((LANES, GLA_QK)), const((GLA_QK, LANES)), const((1, GLA_QK)), const((GLA_QK, 1)),
                  const((1, GLA_DV)),
                  pl.BlockSpec((1, GLA_HEADS, GLA_DK, GLA_DV), lambda b, c: (b, 0, 0, 0))],
        out_specs=[pl.BlockSpec((rows, GLA_V), lambda b, c: (b * steps + c, 0)),
                   pl.BlockSpec((1, GLA_HEADS, GLA_DK, GLA_DV), lambda b, c: (b, 0, 0, 0))],
        out_shape=[jax.ShapeDtypeStruct((nseq * seqlen, GLA_V), F32),
                   jax.ShapeDtypeStruct((nseq, GLA_HEADS, GLA_DK, GLA_DV), F32)],
        scratch_shapes=[pltpu.VMEM((GLA_HEADS, GLA_DK, GLA_DV), F32),
                        pltpu.VMEM((rows + pad, GLA_V), F32)],
        compiler_params=_cparams(("parallel", "arbitrary")),
        name="gla",
    )(h, h, h, h, h, w2, w2t, brow, bcol, g, s0)
    return o, sfin


def _moba_prompt_kernel(q_ref, k_ref, v_ref, o_ref, kt_ref, vt_ref, kb_ref, vtb_ref, km_ref, sel_ref, s_ref,
                        *, nblk):
    i = pl.program_id(2)
    blk = MOBA_BLOCK

    @pl.when(i == 0)
    def _():
        km_ref[...] = jnp.zeros((nblk, LANES), F32)

    kf = k_ref[...]
    kt = kf.T
    vt = v_ref[...].T
    k_own = kf.astype(BF16)
    kb_ref[i] = k_own
    vtb_ref[i] = vt.astype(BF16)
    for pg in range(blk // PAGE_SIZE):
        kt_ref[0, pg] = kt[:, pg * PAGE_SIZE:(pg + 1) * PAGE_SIZE].reshape(2, MOBA_DH, PAGE_SIZE)
        vt_ref[0, pg] = vt[:, pg * PAGE_SIZE:(pg + 1) * PAGE_SIZE].reshape(2, MOBA_DH, PAGE_SIZE)

    qt = (q_ref[...] * np.float32(MOBA_DH ** -0.5)).T
    drow = lax.broadcasted_iota(jnp.int32, (LANES, blk), 0)
    brow = lax.broadcasted_iota(jnp.int32, (nblk, blk), 0)
    krow = lax.broadcasted_iota(jnp.int32, (blk, blk), 0)
    qcol = lax.broadcasted_iota(jnp.int32, (blk, blk), 1)
    km = km_ref[...]

    qbs, maxes = [], []
    for e in range(2):
        qm = jnp.where((drow // MOBA_DH) == e, qt, 0.0)
        gate = _dot(km, qm, HIGHEST)
        gsel = jnp.where(brow < i, gate, NEG_INF)
        sel = jnp.zeros((nblk, blk), F32)
        for _ in range(MOBA_TOPK):
            m = jnp.max(gsel, axis=0, keepdims=True)
            idx = jnp.min(jnp.where(gsel == m, brow, nblk), axis=0, keepdims=True)
            hit = brow == idx
            sel = jnp.where(hit & (m > NEG_INF), 1.0, sel)
            gsel = jnp.where(hit, NEG_INF, gsel)
        sel_ref[e] = sel
        qb = qm.astype(BF16)
        s = jnp.where(krow <= qcol, _dot(k_own, qb), NEG_INF)
        s_ref[e, i] = s
        maxes.append(jnp.max(s, axis=0, keepdims=True))
        qbs.append(qb)

    km_ref[pl.ds(i, 1), :] = jnp.sum(kf, axis=0, keepdims=True) * np.float32(1.0 / blk)

    for e in range(2):
        s_ref[e, i + 1] = jnp.full((blk, blk), NEG_INF, F32)
    vtb_ref[i + 1] = jnp.zeros((LANES, blk), BF16)

    def score_body(t, ms):
        ms = list(ms)
        for j in (2 * t, jnp.minimum(2 * t + 1, i - 1)):
            kj = kb_ref[j]
            for e in range(2):
                s = jnp.where(sel_ref[e, pl.ds(j, 1), :] > 0.0, _dot(kj, qbs[e]), NEG_INF)
                s_ref[e, j] = s
                ms[e] = jnp.maximum(ms[e], jnp.max(s, axis=0, keepdims=True))
        return tuple(ms)

    maxes = lax.fori_loop(0, (i + 1) // 2, score_body, tuple(maxes))

    ones = jnp.ones((2 * SUBLANES, blk), BF16)

    def pv_body(t, accs):
        accs = list(accs)
        for j in (2 * t, 2 * t + 1):
            vtj = vtb_ref[j]
            for e in range(2):
                p = jnp.exp((s_ref[e, j] - maxes[e]).astype(BF16))
                lhs = jnp.concatenate([vtj[e * MOBA_DH:(e + 1) * MOBA_DH], ones], axis=0)
                accs[e] = accs[e] + _dot(lhs, p)
        return tuple(accs)

    zero = jnp.zeros((MOBA_DH + 2 * SUBLANES, blk), F32)
    accs = lax.fori_loop(0, (i + 2) // 2, pv_body, (zero, zero))
    outs = [a[:MOBA_DH] / a[MOBA_DH:MOBA_DH + 1] for a in accs]
    o_ref[...] = jnp.concatenate(outs, axis=0).T


def _moba_prompt(h, nseq, seqlen):
    nblk = seqlen // MOBA_BLOCK
    npair = MOBA_HEADS // 2
    ppb = MOBA_BLOCK // PAGE_SIZE
    o, kt, vt = pl.pallas_call(
        functools.partial(_moba_prompt_kernel, nblk=nblk),
        grid=(nseq, npair, nblk),
        in_specs=[pl.BlockSpec((MOBA_BLOCK, LANES), lambda b, p, i: (b * nblk + i, COL_QB + p)),
                  pl.BlockSpec((MOBA_BLOCK, LANES), lambda b, p, i: (b * nblk + i, COL_KB + p)),
                  pl.BlockSpec((MOBA_BLOCK, LANES), lambda b, p, i: (b * nblk + i, COL_VB + p))],
        out_specs=[pl.BlockSpec((MOBA_BLOCK, LANES), lambda b, p, i: (b * nblk + i, p)),
                   pl.BlockSpec((1, ppb, 2, MOBA_DH, PAGE_SIZE), lambda b, p, i: (b, i, p, 0, 0)),
                   pl.BlockSpec((1, ppb, 2, MOBA_DH, PAGE_SIZE), lambda b, p, i: (b, i, p, 0, 0))],
        out_shape=[jax.ShapeDtypeStruct((nseq * seqlen, MOBA_W), F32),
                   jax.ShapeDtypeStruct((nseq, seqlen // PAGE_SIZE, MOBA_HEADS, MOBA_DH, PAGE_SIZE), F32),
                   jax.ShapeDtypeStruct((nseq, seqlen // PAGE_SIZE, MOBA_HEADS, MOBA_DH, PAGE_SIZE), F32)],
        scratch_shapes=[pltpu.VMEM((nblk, MOBA_BLOCK, LANES), BF16),
                        pltpu.VMEM((nblk + 1, LANES, MOBA_BLOCK), BF16),
                        pltpu.VMEM((nblk, LANES), F32),
                        pltpu.VMEM((2, nblk, MOBA_BLOCK), F32),
                        pltpu.VMEM((2, nblk + 1, MOBA_BLOCK, MOBA_BLOCK), F32)],
        compiler_params=_cparams(("parallel", "parallel", "arbitrary")),
        name="moba_prompt",
    )(h, h, h)
    return o, kt, vt


MOBA_RING = 3
MOBA_GROUP_MAX = 8


def _moba_group(npages):
    return min(MOBA_GROUP_MAX, npages // 2)


def _moba_sample_kernel(pt_ref, q_ref, kn_ref, vn_ref, kc_ref, vc_ref, o_ref,
                        buf_ref, sem_ref, sall_ref, *, layer, npages, tq):
    b = pl.program_id(0)
    hd = MOBA_HEADS * MOBA_DH
    nrow = MOBA_HEADS * tq
    ppb = MOBA_BLOCK // PAGE_SIZE
    nfull = npages // ppb
    grp = _moba_group(npages)
    ngk = npages // grp
    ahead = MOBA_RING - 1

    def copy(src_ref, page, slot):
        return pltpu.make_async_copy(src_ref.at[layer, pt_ref[b, page]], buf_ref.at[slot], sem_ref.at[slot])

    def start_group(src_ref, g_local, g_stream):
        for k in range(grp):
            copy(src_ref, g_local * grp + k, (g_stream % MOBA_RING) * grp + k).start()

    def wait_group(src_ref, g_local, g_stream):
        for k in range(grp):
            copy(src_ref, g_local * grp + k, (g_stream % MOBA_RING) * grp + k).wait()

    for g in range(ahead):
        start_group(kc_ref, g, g)

    qs = q_ref[...] * np.float32(MOBA_DH ** -0.5)
    qt = jnp.concatenate([qs] * MOBA_HEADS, axis=0)
    r_i = lax.broadcasted_iota(jnp.int32, (nrow, hd), 0)
    c_i = lax.broadcasted_iota(jnp.int32, (nrow, hd), 1)
    diag = (r_i // tq) == (c_i // MOBA_DH)
    qbd = jnp.where(diag, qt, 0.0)
    qbd_b = qbd.astype(BF16)
    gl = lax.broadcasted_iota(jnp.int32, (nrow, LANES), 1)

    def k_group(g, carry, next_src):
        gsum, pmax = carry
        nxt = g + ahead
        start_group(next_src, nxt if next_src is kc_ref else nxt - ngk, nxt)
        wait_group(kc_ref, g, g)
        for k in range(grp):
            j = g * grp + k
            kp = buf_ref[(g % MOBA_RING) * grp + k].reshape(hd, PAGE_SIZE)
            s = _dot(qbd_b, kp.astype(BF16))
            sall_ref[j] = s
            gsum = gsum + jnp.where(gl == j // ppb, jnp.sum(s, -1, keepdims=True), 0.0)
            pmax = jnp.where(gl == j, jnp.max(s, -1, keepdims=True), pmax)
        return gsum, pmax

    carry = (jnp.zeros((nrow, LANES), F32), jnp.full((nrow, LANES), NEG_INF, F32))
    carry = lax.fori_loop(0, ngk - ahead, lambda g, c: k_group(g, c, kc_ref), carry)
    for g in range(ngk - ahead, ngk):
        carry = k_group(g, carry, vc_ref)
    gsum, pmax = carry

    gate = gsum * np.float32(1.0 / MOBA_BLOCK)
    gsel = jnp.where(gl < nfull, gate, NEG_INF)
    sel = jnp.zeros((nrow, LANES), F32)
    for _ in range(min(MOBA_TOPK, nfull)):
        m = jnp.max(gsel, -1, keepdims=True)
        idx = jnp.min(jnp.where(gsel == m, gl, LANES), -1, keepdims=True)
        hit = gl == idx
        sel = jnp.where(hit & (m > NEG_INF), 1.0, sel)
        gsel = jnp.where(hit, NEG_INF, gsel)

    kn = jnp.concatenate([kn_ref[...], jnp.zeros((LANES - tq, hd), F32)], axis=0).astype(BF16)
    vn = jnp.concatenate([vn_ref[...], jnp.zeros((LANES - tq, hd), F32)], axis=0).astype(BF16)
    s_new = _dot_nt(qbd_b, kn)
    nr = lax.broadcasted_iota(jnp.int32, (nrow, LANES), 0)
    s_new = jnp.where(gl <= (nr % tq), s_new, NEG_INF)

    eb_r = lax.broadcasted_iota(jnp.int32, (LANES, LANES), 0)
    eb_c = lax.broadcasted_iota(jnp.int32, (LANES, LANES), 1)
    expand = jnp.where((eb_c // ppb == eb_r) & (eb_c < npages), 1.0, 0.0).astype(BF16)
    selp = _dot(sel.astype(BF16), expand)
    m_all = jnp.maximum(jnp.max(jnp.where(selp > 0.0, pmax, NEG_INF), -1, keepdims=True),
                        jnp.max(s_new, -1, keepdims=True))
    p_new = jnp.exp(s_new - m_all)
    l_new = jnp.sum(p_new, -1, keepdims=True)
    acc0 = _dot(p_new.astype(BF16), vn)

    def shift_body(j, carry):
        col = jnp.max(jnp.where(gl == j, selp, 0.0), -1, keepdims=True)
        sall_ref[j] = jnp.where(col > 0.0, sall_ref[j] - m_all, NEG_INF)
        return carry

    lax.fori_loop(0, npages, shift_body, 0, unroll=4)

    def v_group(g, carry, prefetch):
        lsum, acc = carry
        if prefetch:
            start_group(vc_ref, g + ahead, ngk + g + ahead)
        wait_group(vc_ref, g, ngk + g)
        for k in range(grp):
            j = g * grp + k
            vp = buf_ref[((ngk + g) % MOBA_RING) * grp + k].reshape(hd, PAGE_SIZE).astype(BF16)
            p = jnp.exp(sall_ref[j])
            lsum = lsum + p
            acc = acc + _dot_nt(p.astype(BF16), vp)
        return lsum, acc

    carry = (jnp.zeros((nrow, LANES), F32), acc0)
    carry = lax.fori_loop(0, ngk - ahead, lambda g, c: v_group(g, c, True), carry)
    for g in range(ngk - ahead, ngk):
        carry = v_group(g, carry, False)
    lsum, acc = carry
    l_all = l_new + jnp.sum(lsum, -1, keepdims=True)
    res = jnp.where(diag, acc / l_all, 0.0).reshape(MOBA_HEADS, tq, hd)
    o_ref[...] = jnp.sum(res, axis=0)


def _moba_sample(h, page_table, cache_kt, cache_vt, layer, nseq, tq):
    npages = page_table.shape[1]
    hd = MOBA_W
    grp = _moba_group(npages)
    nslot = MOBA_RING * grp
    assert npages <= LANES and npages % (MOBA_BLOCK // PAGE_SIZE) == 0
    assert npages % grp == 0 and npages // grp >= MOBA_RING - 1
    grid_spec = pltpu.PrefetchScalarGridSpec(
        num_scalar_prefetch=1,
        grid=(nseq,),
        in_specs=[pl.BlockSpec((tq, hd), lambda b, pt: (b, COL_QB // 4)),
                  pl.BlockSpec((tq, hd), lambda b, pt: (b, COL_KB // 4)),
                  pl.BlockSpec((tq, hd), lambda b, pt: (b, COL_VB // 4)),
                  pl.BlockSpec(memory_space=pl.ANY),
                  pl.BlockSpec(memory_space=pl.ANY)],
        out_specs=pl.BlockSpec((tq, hd), lambda b, pt: (b, 0)),
        scratch_shapes=[pltpu.VMEM((nslot, MOBA_HEADS, MOBA_DH, PAGE_SIZE), F32),
                        pltpu.SemaphoreType.DMA((nslot,)),
                        pltpu.VMEM((npages, MOBA_HEADS * tq, PAGE_SIZE), F32)],
    )
    return pl.pallas_call(
        functools.partial(_moba_sample_kernel, layer=layer, npages=npages, tq=tq),
        grid_spec=grid_spec,
        out_shape=jax.ShapeDtypeStruct((nseq * tq, hd), F32),
        compiler_params=_cparams(("arbitrary",)),
        name="moba_sample",
    )(page_table, h, h, h, cache_kt, cache_vt)


def _sgu_kernel(u_ref, v_ref, g_ref, b_ref, w_ref, bias_ref, o_ref, *maybe_v_out, period):
    rows = u_ref.shape[0]
    u = _gelu_exact(u_ref[...])
    vn = _layer_norm(_gelu_exact(v_ref[...]), g_ref[...], b_ref[...])
    if maybe_v_out:
        maybe_v_out[0][...] = vn
    t = lax.broadcasted_iota(jnp.int32, (rows, rows), 0)
    s = lax.broadcasted_iota(jnp.int32, (rows, rows), 1)
    keep = ((t // period) == (s // period)) & ((s % period) <= (t % period))
    vb = vn.astype(BF16)
    mix = []
    for g in range(SG_GROUPS):
        wg = jnp.where(keep, w_ref[g], 0.0).astype(BF16)
        mix.append(_dot(wg, vb[:, g * SG_GC:(g + 1) * SG_GC]))
    o_ref[...] = u * (jnp.concatenate(mix, axis=1) + bias_ref[...])


def _sgu(h, ln_g, ln_b, wmix, bias, rows, period, want_v):
    n = h.shape[0]
    const = lambda shp: pl.BlockSpec(shp, lambda i: (0,) * len(shp))
    out_specs = [pl.BlockSpec((rows, SG_W), lambda i: (i, 0))]
    out_shape = [jax.ShapeDtypeStruct((n, SG_W), F32)]
    if want_v:
        out_specs.append(pl.BlockSpec((rows, SG_W), lambda i: (i, 0)))
        out_shape.append(jax.ShapeDtypeStruct((n, SG_W), F32))
    res = pl.pallas_call(
        functools.partial(_sgu_kernel, period=period),
        grid=(n // rows,),
        in_specs=[pl.BlockSpec((rows, SG_W), lambda i: (i, 6)),
                  pl.BlockSpec((rows, SG_W), lambda i: (i, 7)),
                  const((1, SG_W)), const((1, SG_W)),
                  const((SG_GROUPS, rows, rows)), const((rows, SG_W))],
        out_specs=out_specs,
        out_shape=out_shape,
        compiler_params=_cparams(("parallel",)),
        name="sgu",
    )(h, h, ln_g, ln_b, wmix, bias)
    return res if want_v else (res[0], None)


def _merge_kernel(oa_ref, ob_ref, oc_ref, ga_ref, gb_ref, gc_ref, x_ref, pa_ref, pb_ref, pc_ref, wo_ref,
                  g_ref, b_ref, o_ref):
    m = (jax.nn.sigmoid(ga_ref[...]) * _dot(oa_ref[...].astype(BF16), pa_ref[...])
         + jax.nn.sigmoid(gb_ref[...]) * _dot(ob_ref[...].astype(BF16), pb_ref[...])
         + jax.nn.sigmoid(gc_ref[...]) * _dot(oc_ref[...].astype(BF16), pc_ref[...]))
    y = np.float32(ALPHA) * x_ref[...] + _dot(m.astype(BF16), wo_ref[...])
    o_ref[...] = _layer_norm(y, g_ref[...], b_ref[...])


def _merge(oa, ob, oc, h, x, pa, pb, pc, wo, g, b):
    n = x.shape[0]
    tm = min(n, 512)
    row = lambda w, j: pl.BlockSpec((tm, w), lambda i: (i, j))
    const = lambda shp: pl.BlockSpec(shp, lambda i: (0,) * len(shp))
    return pl.pallas_call(
        _merge_kernel,
        grid=(n // tm,),
        in_specs=[row(GLA_V, 0), row(MOBA_W, 0), row(SG_W, 0),
                  row(D_MODEL, 4), row(D_MODEL, 5), row(D_MODEL, 6), row(D_MODEL, 0),
                  const((GLA_V, D_MODEL)), const((MOBA_W, D_MODEL)), const((SG_W, D_MODEL)),
                  const((D_MODEL, D_MODEL)), const((1, D_MODEL)), const((1, D_MODEL))],
        out_specs=row(D_MODEL, 0),
        out_shape=jax.ShapeDtypeStruct((n, D_MODEL), F32),
        compiler_params=_cparams(("parallel",)),
        name="merge",
    )(oa, ob, oc, h, h, h, x, pa, pb, pc, wo, g, b)


MOE_CHUNK_EXPERTS = 8


def _moe_kernel(x_ref, wr_ref, br_ref, wg_ref, wu_ref, wd_ref, sg_ref, su_ref, sd_ref, g_ref, b_ref,
                o_ref, xb_ref, gates_ref, acc_ref):
    c = pl.program_id(1)
    nchunk = pl.num_programs(1)
    tm = x_ref.shape[0]
    ce = MOE_CHUNK_EXPERTS
    gsz = N_EXPERTS // N_EXPERT_GROUPS

    @pl.when(c == 0)
    def _():
        x = x_ref[...]
        xb = x.astype(BF16)
        xb_ref[...] = xb
        s = jax.nn.sigmoid(_dot_nt(wr_ref[...], x, HIGHEST))
        sel = s + br_ref[...]
        sel3 = sel.reshape(N_EXPERT_GROUPS, gsz, tm)
        j3 = lax.broadcasted_iota(jnp.int32, sel3.shape, 1)
        m1 = jnp.max(sel3, axis=1, keepdims=True)
        i1 = jnp.min(jnp.where(sel3 == m1, j3, gsz), axis=1, keepdims=True)
        m2 = jnp.max(jnp.where(j3 == i1, NEG_INF, sel3), axis=1, keepdims=True)
        grp = (m1 + m2).reshape(N_EXPERT_GROUPS, tm)
        gi = lax.broadcasted_iota(jnp.int32, grp.shape, 0)
        gkeep = jnp.zeros(grp.shape, F32)
        for _ in range(TOPK_GROUPS):
            m = jnp.max(grp, axis=0, keepdims=True)
            idx = jnp.min(jnp.where(grp == m, gi, N_EXPERT_GROUPS), axis=0, keepdims=True)
            hit = gi == idx
            gkeep = jnp.where(hit, 1.0, gkeep)
            grp = jnp.where(hit, NEG_INF, grp)
        emask = jnp.broadcast_to(gkeep.reshape(N_EXPERT_GROUPS, 1, tm), sel3.shape).reshape(N_EXPERTS, tm)
        cand = jnp.where(emask > 0.0, sel, NEG_INF)
        ei = lax.broadcasted_iota(jnp.int32, cand.shape, 0)
        chosen = jnp.zeros(cand.shape, F32)
        for _ in range(TOP_K):
            m = jnp.max(cand, axis=0, keepdims=True)
            idx = jnp.min(jnp.where(cand == m, ei, N_EXPERTS), axis=0, keepdims=True)
            hit = ei == idx
            chosen = jnp.where(hit, 1.0, chosen)
            cand = jnp.where(hit, NEG_INF, cand)
        w = jnp.where(chosen > 0.0, s, 0.0)
        gates_ref[...] = (w / jnp.sum(w, axis=0, keepdims=True) * np.float32(ROUTED_SCALE)).reshape(
            N_EXPERTS // ce, ce, tm)
        hs = _silu(_dot(xb, sg_ref[...])) * _dot(xb, su_ref[...])
        acc_ref[...] = _dot(hs.astype(BF16), sd_ref[...])

    xb = xb_ref[...]
    hh = _silu(_dot(xb, wg_ref[...])) * _dot(xb, wu_ref[...])
    gt = jnp.concatenate([gates_ref[c], jnp.zeros((LANES - ce, tm), F32)], axis=0).T
    gexp = jnp.concatenate([jnp.broadcast_to(gt[:, j:j + 1], (tm, D_EXPERT)) for j in range(ce)], axis=1)
    acc_ref[...] += _dot((hh * gexp).astype(BF16), wd_ref[...])

    @pl.when(c == nchunk - 1)
    def _():
        o_ref[...] = _layer_norm(np.float32(ALPHA) * x_ref[...] + acc_ref[...], g_ref[...], b_ref[...])


def _moe(x, wr_t, br, wg, wu, wd, sg, su, sd, g, b):
    n = x.shape[0]
    tm = min(n, 512)
    cw = MOE_CHUNK_EXPERTS * D_EXPERT
    nchunk = N_EXPERTS // MOE_CHUNK_EXPERTS
    const = lambda shp: pl.BlockSpec(shp, lambda i, c: (0,) * len(shp))
    return pl.pallas_call(
        _moe_kernel,
        grid=(n // tm, nchunk),
        in_specs=[pl.BlockSpec((tm, D_MODEL), lambda i, c: (i, 0)),
                  const((N_EXPERTS, D_MODEL)), const((N_EXPERTS, 1)),
                  pl.BlockSpec((D_MODEL, cw), lambda i, c: (0, c)),
                  pl.BlockSpec((D_MODEL, cw), lambda i, c: (0, c)),
                  pl.BlockSpec((cw, D_MODEL), lambda i, c: (c, 0)),
                  const((D_MODEL, D_SHARED)), const((D_MODEL, D_SHARED)), const((D_SHARED, D_MODEL)),
                  const((1, D_MODEL)), const((1, D_MODEL))],
        out_specs=pl.BlockSpec((tm, D_MODEL), lambda i, c: (i, 0)),
        out_shape=jax.ShapeDtypeStruct((n, D_MODEL), F32),
        scratch_shapes=[pltpu.VMEM((tm, D_MODEL), BF16),
                        pltpu.VMEM((nchunk, MOE_CHUNK_EXPERTS, tm), F32),
                        pltpu.VMEM((tm, D_MODEL), F32)],
        compiler_params=_cparams(("parallel", "arbitrary")),
        name="moe",
    )(x, wr_t, br, wg, wu, wd, sg, su, sd, g, b)


def _prep_weights(w_in, gla_w_a2, gla_b_a2, gla_norm_g, sg_ln_g, sg_ln_b, sg_w, sg_b, w_branch_a, w_branch_b,
                  w_branch_c, w_out, ln1_g, ln1_b, ln2_g, ln2_b, moe_w_router, moe_b_router, moe_w_gate,
                  moe_w_up, moe_w_down, sh_w_gate, sh_w_up, sh_w_down, dec_seq):
    depth = w_in.shape[0]
    c_alr = 2 * GLA_QK + 2 * GLA_V
    w_perm = jnp.concatenate(
        [w_in[:, :, :c_alr], w_in[:, :, c_alr + GLA_RANK:], w_in[:, :, c_alr:c_alr + GLA_RANK],
         jnp.zeros((depth, D_MODEL, LANES - GLA_RANK), w_in.dtype)], axis=2).astype(BF16)
    w2 = jnp.concatenate([gla_w_a2, jnp.zeros((depth, LANES - GLA_RANK, GLA_QK), F32)], axis=1).astype(BF16)
    reps = PAGE_SIZE // dec_seq if dec_seq < SG_CHUNK else 1
    nrow_s = 256
    tile_s = nrow_s // dec_seq
    return dict(
        w_perm=w_perm,
        w2=w2, w2t=jnp.swapaxes(w2, 1, 2),
        brow=gla_b_a2[:, None, :], bcol=gla_b_a2[:, :, None],
        gla_g=gla_norm_g[:, None, :],
        sg_g=sg_ln_g[:, None, :], sg_b=sg_ln_b[:, None, :],
        sg_w_p=sg_w,
        sg_bias_p=jnp.repeat(jnp.swapaxes(sg_b, 1, 2), SG_GC, axis=2),
        sg_w_s=jnp.tile(sg_w[:, :, :dec_seq, :dec_seq], (1, 1, tile_s, tile_s)),
        sg_bias_s=jnp.tile(jnp.repeat(jnp.swapaxes(sg_b[:, :, :dec_seq], 1, 2), SG_GC, axis=2), (1, tile_s, 1)),
        pa=w_branch_a.astype(BF16), pb=w_branch_b.astype(BF16), pc=w_branch_c.astype(BF16),
        wo=w_out.astype(BF16),
        ln1_g=ln1_g[:, None, :], ln1_b=ln1_b[:, None, :], ln2_g=ln2_g[:, None, :], ln2_b=ln2_b[:, None, :],
        wr_t=jnp.swapaxes(moe_w_router, 1, 2), br=moe_b_router[:, :, None],
        wg=jnp.transpose(moe_w_gate, (0, 2, 1, 3)).reshape(depth, D_MODEL, N_EXPERTS * D_EXPERT).astype(BF16),
        wu=jnp.transpose(moe_w_up, (0, 2, 1, 3)).reshape(depth, D_MODEL, N_EXPERTS * D_EXPERT).astype(BF16),
        wd=moe_w_down.reshape(depth, N_EXPERTS * D_EXPERT, D_MODEL).astype(BF16),
        sg=sh_w_gate.astype(BF16), su=sh_w_up.astype(BF16), sd=sh_w_down.astype(BF16),
    )


def _token_tail(x, h, oa, ob, oc, w, l):
    x1 = _merge(oa, ob, oc, h, x, w["pa"][l], w["pb"][l], w["pc"][l], w["wo"][l], w["ln1_g"][l], w["ln1_b"][l])
    return _moe(x1, w["wr_t"][l], w["br"][l], w["wg"][l], w["wu"][l], w["wd"][l], w["sg"][l], w["su"][l],
                w["sd"][l], w["ln2_g"][l], w["ln2_b"][l])


def kernel(x_prompt, x_sample, cache_k, cache_v, state_gla, page_table, w_in, gla_w_a2, gla_b_a2, gla_norm_g,
           sg_ln_g, sg_ln_b, sg_w, sg_b, w_branch_a, w_branch_b, w_branch_c, w_out, ln1_g, ln1_b, ln2_g, ln2_b,
           moe_w_router, moe_b_router, moe_w_gate, moe_w_up, moe_w_down, sh_w_gate, sh_w_up, sh_w_down):
    bp, sp, d = x_prompt.shape
    db, t, _ = x_sample.shape
    depth = w_in.shape[0]
    assert d == D_MODEL and sp % MOBA_BLOCK == 0 and sp % SG_CHUNK == 0 and t <= SUBLANES
    assert (db * t) % 256 == 0 or db * t == 256
    w = _prep_weights(w_in, gla_w_a2, gla_b_a2, gla_norm_g, sg_ln_g, sg_ln_b, sg_w, sg_b, w_branch_a,
                      w_branch_b, w_branch_c, w_out, ln1_g, ln1_b, ln2_g, ln2_b, moe_w_router, moe_b_router,
                      moe_w_gate, moe_w_up, moe_w_down, sh_w_gate, sh_w_up, sh_w_down, t)
    cache_kt = jnp.swapaxes(cache_k, 3, 4)
    cache_vt = jnp.swapaxes(cache_v, 3, 4)
    xp = x_prompt.reshape(bp * sp, d)
    xs = x_sample.reshape(db * t, d)
    zero_state = jnp.zeros((bp, GLA_HEADS, GLA_DK, GLA_DV), F32)
    gla_p, gla_s, kp_l, vp_l, ks_l, vs_l, sgu_l = [], [], [], [], [], [], []
    for l in range(depth):
        gla_args = (w["w2"][l], w["w2t"][l], w["brow"][l], w["bcol"][l], w["gla_g"][l])
        hp = _inproj(xp, w["w_perm"][l])
        oa, s_p = _gla(hp, zero_state, *gla_args, bp, sp)
        ob, kt, vt = _moba_prompt(hp, bp, sp)
        oc, _ = _sgu(hp, w["sg_g"][l], w["sg_b"][l], w["sg_w_p"][l], w["sg_bias_p"][l], SG_CHUNK, SG_CHUNK, False)
        xp = _token_tail(xp, hp, oa, ob, oc, w, l)
        hs = _inproj(xs, w["w_perm"][l])
        oa, s_s = _gla(hs, state_gla[l], *gla_args, db, t)
        ob = _moba_sample(hs, page_table, cache_kt, cache_vt, l, db, t)
        oc, vcs = _sgu(hs, w["sg_g"][l], w["sg_b"][l], w["sg_w_s"][l], w["sg_bias_s"][l], db * t, t, True)
        xs = _token_tail(xs, hs, oa, ob, oc, w, l)
        gla_p.append(s_p)
        gla_s.append(s_s)
        kp_l.append(jnp.swapaxes(kt, 3, 4))
        vp_l.append(jnp.swapaxes(vt, 3, 4))
        kb_s = hs[:, COL_KB * LANES:COL_KB * LANES + MOBA_W].reshape(db, t, MOBA_HEADS, MOBA_DH)
        vb_s = hs[:, COL_VB * LANES:COL_VB * LANES + MOBA_W].reshape(db, t, MOBA_HEADS, MOBA_DH)
        ks_l.append(kb_s.transpose(0, 2, 1, 3))
        vs_l.append(vb_s.transpose(0, 2, 1, 3))
        sgu_l.append(vcs.reshape(db, t, SG_W))
    return (xp.reshape(bp, sp, d), xs.reshape(db, t, d), jnp.stack(gla_p), jnp.stack(gla_s),
            jnp.stack(kp_l), jnp.stack(vp_l), jnp.stack(ks_l), jnp.stack(vs_l), jnp.stack(sgu_l))
```

```python
import functools
import math

import numpy as np
import jax
import jax.numpy as jnp
from jax import lax
from jax.experimental import pallas as pl
from jax.experimental.pallas import tpu as pltpu

F32 = jnp.float32
BF16 = jnp.bfloat16

D_MODEL = 1024
DEPTH = 4
PAGE_SIZE = 128
GLA_HEADS, GLA_DK, GLA_DV, GLA_RANK, GLA_TAU = 4, 64, 128, 16, 16.0
GLA_QK, GLA_V = GLA_HEADS * GLA_DK, GLA_HEADS * GLA_DV
MOBA_HEADS, MOBA_DH, MOBA_BLOCK, MOBA_TOPK = 8, 64, 256, 3
MOBA_W = MOBA_HEADS * MOBA_DH
SG_GROUPS, SG_GC, SG_CHUNK = 4, 128, 128
SG_W = SG_GROUPS * SG_GC
N_EXPERTS, TOP_K, N_EXPERT_GROUPS, TOPK_GROUPS = 64, 8, 8, 4
D_EXPERT, D_SHARED, ROUTED_SCALE = 128, 128, 2.5
ALPHA = (2 * DEPTH) ** 0.25
LN_EPS = 1e-5

LANES = 128
SUBLANES = 8
VMEM_LIMIT_BYTES = 56 * 1024 * 1024

N_IN_PERM = 2 * GLA_QK + 2 * GLA_V + 3 * MOBA_W + 2 * SG_W + 3 * D_MODEL + LANES
COL_QB, COL_KB, COL_VB = 12, 16, 20
COL_ALR = 56
GLA_SUB = 16
GLA_SEQS_PER_STEP = 4
NEG_INF = float("-inf")


def _cparams(sem):
    return pltpu.CompilerParams(dimension_semantics=sem, vmem_limit_bytes=VMEM_LIMIT_BYTES)


def _layer_norm(x, g, b):
    mu = jnp.mean(x, -1, keepdims=True)
    xc = x - mu
    var = jnp.mean(xc * xc, -1, keepdims=True)
    return xc * lax.rsqrt(var + LN_EPS) * g + b


def _log_sigmoid(z):
    return jnp.minimum(z, 0.0) - jnp.log1p(jnp.exp(-jnp.abs(z)))


def _silu(x):
    return x * jax.nn.sigmoid(x)


def _gelu_exact(x):
    return 0.5 * x * (1.0 + lax.erf(x * np.float32(np.sqrt(0.5))))


def _dot(a, b, precision=None):
    return jnp.dot(a, b, preferred_element_type=F32, precision=precision)


def _dot_nt(a, b, precision=None):
    return lax.dot_general(a, b, (((1,), (1,)), ((), ())), preferred_element_type=F32, precision=precision)


def _dot_tn(a, b, precision=None):
    return lax.dot_general(a, b, (((0,), (0,)), ((), ())), preferred_element_type=F32, precision=precision)


HIGHEST = lax.Precision.HIGHEST


def _inproj_kernel(x_ref, w_ref, o_ref, xb_ref):
    @pl.when(pl.program_id(1) == 0)
    def _():
        xb_ref[...] = x_ref[...].astype(BF16)

    o_ref[...] = _dot(xb_ref[...], w_ref[...])


def _inproj(x, w):
    n, d = x.shape
    nc = w.shape[1]
    tm = min(n, 512)
    tn = nc // 3
    return pl.pallas_call(
        _inproj_kernel,
        grid=(n // tm, nc // tn),
        in_specs=[pl.BlockSpec((tm, d), lambda i, j: (i, 0)),
                  pl.BlockSpec((d, tn), lambda i, j: (0, j))],
        out_specs=pl.BlockSpec((tm, tn), lambda i, j: (i, j)),
        out_shape=jax.ShapeDtypeStruct((n, nc), F32),
        scratch_shapes=[pltpu.VMEM((tm, d), BF16)],
        compiler_params=_cparams(("parallel", "arbitrary")),
        name="inproj",
    )(x, w)


def _gla_kernel(q_ref, k_ref, v_ref, r_ref, alr_ref, w2_ref, w2t_ref, brow_ref, bcol_ref, g_ref, s0_ref,
                o_ref, sfin_ref, s_ref, oraw_ref, *, nb, rows, pad):
    c = pl.program_id(1)
    nsteps = pl.num_programs(1)
    rr = rows + pad
    nblk = rr // GLA_SUB

    @pl.when(c == 0)
    def _():
        s_ref[...] = s0_ref[...]

    def padrows(a):
        if pad == 0:
            return a
        return jnp.concatenate([a, jnp.zeros((pad, a.shape[1]), a.dtype)], axis=0)

    ti = lax.broadcasted_iota(jnp.int32, (rr, rr), 0)
    si = lax.broadcasted_iota(jnp.int32, (rr, rr), 1)
    same = (ti // GLA_SUB) == (si // GLA_SUB)
    m_incl = jnp.where(same & (si <= ti), 1.0, 0.0).astype(F32)
    m_blk = jnp.where(same, 1.0, 0.0).astype(F32)
    bi_r = lax.broadcasted_iota(jnp.int32, (rr, LANES), 0)
    bi_c = lax.broadcasted_iota(jnp.int32, (rr, LANES), 1)
    m_ind = jnp.where((bi_r // GLA_SUB) == bi_c, 1.0, 0.0).astype(F32)
    he_r = lax.broadcasted_iota(jnp.int32, (GLA_QK, GLA_V), 0)
    he_c = lax.broadcasted_iota(jnp.int32, (GLA_QK, GLA_V), 1)
    he = jnp.where((he_r // GLA_DK) == (he_c // GLA_DV), 1.0, 0.0).astype(BF16)
    trow = lax.broadcasted_iota(jnp.int32, (GLA_SUB, 1), 0)

    seqs = []
    for b in range(nb):
        q = padrows(q_ref[b]) * np.float32(GLA_DK ** -0.5)
        k = padrows(k_ref[b])
        v = padrows(v_ref[b])
        alr = padrows(alr_ref[b]).astype(BF16)
        la = _log_sigmoid(_dot(alr, w2_ref[...]) + brow_ref[...]) * np.float32(1.0 / GLA_TAU)
        lat = _log_sigmoid(_dot_nt(w2t_ref[...], alr) + bcol_ref[...]) * np.float32(1.0 / GLA_TAU)
        if pad:
            la = jnp.where(lax.broadcasted_iota(jnp.int32, la.shape, 0) < rows, la, 0.0)
            lat = jnp.where(lax.broadcasted_iota(jnp.int32, lat.shape, 1) < rows, lat, 0.0)
        b_loc = _dot(m_incl, la, HIGHEST)
        b_tot = _dot(m_blk, la, HIGHEST)
        seqs.append(dict(
            q=q, k=k, v=v, b_loc=b_loc,
            qd=(q * jnp.exp(b_loc)).astype(BF16),
            kd=(k * jnp.exp(b_tot - b_loc)).astype(BF16),
            vb=v.astype(BF16),
            dcol=jnp.exp(_dot(lat, m_ind, HIGHEST)),
            s=[s_ref[b, h] for h in range(GLA_HEADS)]))

    for i in range(nblk):
        r0 = i * GLA_SUB
        for b in range(nb):
            sq = seqs[b]
            b_i = sq["b_loc"][r0:r0 + GLA_SUB]
            q_i = sq["q"][r0:r0 + GLA_SUB]
            k_i = sq["k"][r0:r0 + GLA_SUB]
            v_i = sq["v"][r0:r0 + GLA_SUB]
            half = GLA_SUB // 2
            parts = []
            for s in range(GLA_SUB):
                t0 = 0 if s < half else half
                e = jnp.exp(jnp.minimum(b_i[t0:] - b_i[s:s + 1], 0.0))
                parts.append(jnp.where(trow[t0:] >= s, q_i[t0:] * e * k_i[s:s + 1], 0.0))
            pstack = jnp.concatenate(parts, axis=0).astype(BF16)
            rexp = _dot(pstack, he)
            o_blk = rexp[0:GLA_SUB] * v_i[0:1]
            for s in range(1, half):
                o_blk = o_blk + rexp[s * GLA_SUB:(s + 1) * GLA_SUB] * v_i[s:s + 1]
            base = half * GLA_SUB
            o_hi = rexp[base:base + half] * v_i[half:half + 1]
            for s in range(half + 1, GLA_SUB):
                r1 = base + (s - half) * half
                o_hi = o_hi + rexp[r1:r1 + half] * v_i[s:s + 1]
            o_blk = o_blk + jnp.concatenate([jnp.zeros((half, GLA_V), F32), o_hi], axis=0)
            outs = []
            for h in range(GLA_HEADS):
                ks = slice(h * GLA_DK, (h + 1) * GLA_DK)
                vs = slice(h * GLA_DV, (h + 1) * GLA_DV)
                s_h = sq["s"][h]
                outs.append(_dot(sq["qd"][r0:r0 + GLA_SUB, ks], s_h.astype(BF16)))
                upd = _dot_tn(sq["kd"][r0:r0 + GLA_SUB, ks], sq["vb"][r0:r0 + GLA_SUB, vs])
                sq["s"][h] = sq["dcol"][ks, i:i + 1] * s_h + upd
            oraw_ref[b, r0:r0 + GLA_SUB, :] = o_blk + jnp.concatenate(outs, axis=1)

    g = g_ref[...]
    for b in range(nb):
        for h in range(GLA_HEADS):
            s_ref[b, h] = seqs[b]["s"][h]
        o = oraw_ref[b, 0:rows, :]
        normed = []
        for h in range(GLA_HEADS):
            oh = o[:, h * GLA_DV:(h + 1) * GLA_DV]
            ms = jnp.mean(oh * oh, -1, keepdims=True)
            normed.append(oh * lax.rsqrt(ms + 1e-6) * g)
        o_ref[b] = jnp.concatenate(normed, axis=1) * _silu(r_ref[b])

    @pl.when(c == nsteps - 1)
    def _():
        sfin_ref[...] = s_ref[...]


def _gla(h, s0, w2, w2t, brow, bcol, g, nseq, seqlen, nb):
    rows = min(seqlen, 128)
    pad = (-rows) % GLA_SUB
    steps = seqlen // rows
    h3 = h.reshape(nseq, seqlen, h.shape[1])
    blk = lambda w, j: pl.BlockSpec((nb, rows, w), lambda b, c: (b, c, j))
    const = lambda shp: pl.BlockSpec(shp, lambda b, c: (0,) * len(shp))
    state = pl.BlockSpec((nb, GLA_HEADS, GLA_DK, GLA_DV), lambda b, c: (b, 0, 0, 0))
    o, sfin = pl.pallas_call(
        functools.partial(_gla_kernel, nb=nb, rows=rows, pad=pad),
        grid=(nseq // nb, steps),
        in_specs=[blk(GLA_QK, 0), blk(GLA_QK, 1), blk(GLA_V, 1), blk(GLA_V, 2), blk(LANES, COL_ALR),
                  const((LANES, GLA_QK)), const((GLA_QK, LANES)), const((1, GLA_QK)), const((GLA_QK, 1)),
                  const((1, GLA_DV)), state],
        out_specs=[pl.BlockSpec((nb, rows, GLA_V), lambda b, c: (b, c, 0)), state],
        out_shape=[jax.ShapeDtypeStruct((nseq, seqlen, GLA_V), F32),
                   jax.ShapeDtypeStruct((nseq, GLA_HEADS, GLA_DK, GLA_DV), F32)],
        scratch_shapes=[pltpu.VMEM((nb, GLA_HEADS, GLA_DK, GLA_DV), F32),
                        pltpu.VMEM((nb, rows + pad, GLA_V), F32)],
        compiler_params=_cparams(("parallel", "arbitrary")),
        name="gla",
    )(h3, h3, h3, h3, h3, w2, w2t, brow, bcol, g, s0)
    return o.reshape(nseq * seqlen, GLA_V), sfin


def _moba_prompt_kernel(q_ref, k_ref, v_ref, o_ref, kt_ref, vt_ref, kb_ref, vtb_ref, km_ref, sel_ref, s_ref,
                        *, nblk):
    i = pl.program_id(2)
    blk = MOBA_BLOCK

    @pl.when(i == 0)
    def _():
        km_ref[...] = jnp.zeros((nblk, LANES), F32)
        kb_ref[...] = jnp.zeros(kb_ref.shape, BF16)
        vtb_ref[...] = jnp.zeros(vtb_ref.shape, BF16)

    kf = k_ref[...]
    kt = kf.T
    vt = v_ref[...].T
    k_own = kf.astype(BF16)
    vt_own = vt.astype(BF16)
    kb_ref[i] = k_own
    vtb_ref[i] = vt_own
    for pg in range(blk // PAGE_SIZE):
        kt_ref[0, pg] = kt[:, pg * PAGE_SIZE:(pg + 1) * PAGE_SIZE].reshape(2, MOBA_DH, PAGE_SIZE)
        vt_ref[0, pg] = vt[:, pg * PAGE_SIZE:(pg + 1) * PAGE_SIZE].reshape(2, MOBA_DH, PAGE_SIZE)

    qt = (q_ref[...] * np.float32(MOBA_DH ** -0.5)).T
    drow = lax.broadcasted_iota(jnp.int32, (LANES, blk), 0)
    brow = lax.broadcasted_iota(jnp.int32, (nblk, blk), 0)
    krow = lax.broadcasted_iota(jnp.int32, (blk, blk), 0)
    qcol = lax.broadcasted_iota(jnp.int32, (blk, blk), 1)
    km = km_ref[...]

    ones = jnp.ones((2 * SUBLANES, blk), BF16)

    def lhs_rows(vtj, e):
        return jnp.concatenate([vtj[e * MOBA_DH:(e + 1) * MOBA_DH], ones], axis=0)

    qbs, ms, accs = [], [], []
    for e in range(2):
        qm = jnp.where((drow // MOBA_DH) == e, qt, 0.0)
        gate = _dot(km, qm, HIGHEST)
        gsel = jnp.where(brow < i, gate, NEG_INF)
        sel = jnp.zeros((nblk, blk), F32)
        for _ in range(MOBA_TOPK):
            m = jnp.max(gsel, axis=0, keepdims=True)
            idx = jnp.min(jnp.where(gsel == m, brow, nblk), axis=0, keepdims=True)
            hit = brow == idx
            sel = jnp.where(hit & (m > NEG_INF), 1.0, sel)
            gsel = jnp.where(hit, NEG_INF, gsel)
        sel_ref[e] = sel
        qb = qm.astype(BF16)
        s = jnp.where(krow <= qcol, _dot(k_own, qb), NEG_INF)
        m0 = jnp.max(s, axis=0, keepdims=True)
        p = jnp.exp((s - m0).astype(BF16))
        accs.append(_dot(lhs_rows(vt_own, e), p))
        ms.append(m0)
        qbs.append(qb)

    km_ref[pl.ds(i, 1), :] = jnp.sum(kf, axis=0, keepdims=True) * np.float32(1.0 / blk)

    npairs = (i + 1) // 2

    def stage_a(t, slot):
        mx = []
        for e in range(2):
            mx.append(jnp.full((1, blk), NEG_INF, F32))
        for bb in range(2):
            j = jnp.minimum(2 * t + bb, nblk - 1)
            kj = kb_ref[j]
            for e in range(2):
                s = jnp.where(sel_ref[e, pl.ds(j, 1), :] > 0.0, _dot(kj, qbs[e]), NEG_INF)
                s_ref[e, slot, bb] = s
                mx[e] = jnp.maximum(mx[e], jnp.max(s, axis=0, keepdims=True))
        return mx

    def stage_b(t, slot, ms, accs, mx):
        ms, accs = list(ms), list(accs)
        for e in range(2):
            m_new = jnp.maximum(ms[e], mx[e])
            contrib = None
            for bb in range(2):
                j = jnp.minimum(2 * t + bb, nblk - 1)
                p = jnp.exp((s_ref[e, slot, bb] - m_new).astype(BF16))
                d = _dot(lhs_rows(vtb_ref[j], e), p)
                contrib = d if contrib is None else contrib + d
            accs[e] = jnp.exp(ms[e] - m_new) * accs[e] + contrib
            ms[e] = m_new
        return ms, accs

    def body(u, c):
        ms, accs, mx0 = c[0:2], c[2:4], c[4:6]
        mx1 = stage_a(2 * u + 1, 1)
        ms, accs = stage_b(2 * u, 0, ms, accs, mx0)
        mx0 = stage_a(2 * u + 2, 0)
        ms, accs = stage_b(2 * u + 1, 1, ms, accs, mx1)
        return (*ms, *accs, *mx0)

    nloop = jnp.maximum((npairs - 1) // 2, 0)
    c = lax.fori_loop(0, nloop, body, (*ms, *accs, *stage_a(0, 0)))
    mx1 = stage_a(2 * nloop + 1, 1)
    ms, accs = stage_b(2 * nloop, 0, c[0:2], c[2:4], c[4:6])
    ms, accs = stage_b(2 * nloop + 1, 1, ms, accs, mx1)
    outs = [a[:MOBA_DH] / a[MOBA_DH:MOBA_DH + 1] for a in accs]
    o_ref[...] = jnp.concatenate(outs, axis=0).T


def _moba_prompt(h, nseq, seqlen):
    nblk = seqlen // MOBA_BLOCK
    npair = MOBA_HEADS // 2
    ppb = MOBA_BLOCK // PAGE_SIZE
    o, kt, vt = pl.pallas_call(
        functools.partial(_moba_prompt_kernel, nblk=nblk),
        grid=(nseq, npair, nblk),
        in_specs=[pl.BlockSpec((MOBA_BLOCK, LANES), lambda b, p, i: (b * nblk + i, COL_QB + p)),
                  pl.BlockSpec((MOBA_BLOCK, LANES), lambda b, p, i: (b * nblk + i, COL_KB + p)),
                  pl.BlockSpec((MOBA_BLOCK, LANES), lambda b, p, i: (b * nblk + i, COL_VB + p))],
        out_specs=[pl.BlockSpec((MOBA_BLOCK, LANES), lambda b, p, i: (b * nblk + i, p)),
                   pl.BlockSpec((1, ppb, 2, MOBA_DH, PAGE_SIZE), lambda b, p, i: (b, i, p, 0, 0)),
                   pl.BlockSpec((1, ppb, 2, MOBA_DH, PAGE_SIZE), lambda b, p, i: (b, i, p, 0, 0))],
        out_shape=[jax.ShapeDtypeStruct((nseq * seqlen, MOBA_W), F32),
                   jax.ShapeDtypeStruct((nseq, seqlen // PAGE_SIZE, MOBA_HEADS, MOBA_DH, PAGE_SIZE), F32),
                   jax.ShapeDtypeStruct((nseq, seqlen // PAGE_SIZE, MOBA_HEADS, MOBA_DH, PAGE_SIZE), F32)],
        scratch_shapes=[pltpu.VMEM((nblk, MOBA_BLOCK, LANES), BF16),
                        pltpu.VMEM((nblk, LANES, MOBA_BLOCK), BF16),
                        pltpu.VMEM((nblk, LANES), F32),
                        pltpu.VMEM((2, nblk, MOBA_BLOCK), F32),
                        pltpu.VMEM((2, 2, 2, MOBA_BLOCK, MOBA_BLOCK), F32)],
        compiler_params=_cparams(("parallel", "parallel", "arbitrary")),
        name="moba_prompt",
    )(h, h, h)
    return o, kt, vt


MOBA_RING = 3
MOBA_GROUP_MAX = 8


def _moba_group(npages):
    return min(MOBA_GROUP_MAX, npages // 2)


def _moba_sample_kernel(pt_ref, q_ref, kn_ref, vn_ref, kc_ref, vc_ref, o_ref,
                        buf_ref, sem_ref, sall_ref, *, layer, npages, tq):
    b = pl.program_id(0)
    hd = MOBA_HEADS * MOBA_DH
    nrow = MOBA_HEADS * tq
    ppb = MOBA_BLOCK // PAGE_SIZE
    nfull = npages // ppb
    grp = _moba_group(npages)
    ngk = npages // grp
    ahead = MOBA_RING - 1

    nseq = pl.num_programs(0)
    base = b * (2 * ngk)

    def ring(g_stream):
        return ((base + g_stream) % MOBA_RING) * grp

    def copy(src_ref, seq, page, slot):
        return pltpu.make_async_copy(src_ref.at[layer, pt_ref[seq, page]], buf_ref.at[slot], sem_ref.at[slot])

    def start_group(src_ref, seq, g_local, g_stream):
        for k in range(grp):
            copy(src_ref, seq, g_local * grp + k, ring(g_stream) + k).start()

    def wait_group(src_ref, g_local, g_stream):
        for k in range(grp):
            copy(src_ref, b, g_local * grp + k, ring(g_stream) + k).wait()

    @pl.when(b == 0)
    def _():
        for g in range(ahead):
            start_group(kc_ref, b, g, g)

    qs = q_ref[...] * np.float32(MOBA_DH ** -0.5)
    qt = jnp.concatenate([qs] * MOBA_HEADS, axis=0)
    r_i = lax.broadcasted_iota(jnp.int32, (nrow, hd), 0)
    c_i = lax.broadcasted_iota(jnp.int32, (nrow, hd), 1)
    diag = (r_i // tq) == (c_i // MOBA_DH)
    qbd = jnp.where(diag, qt, 0.0)
    qbd_b = qbd.astype(BF16)
    gl = lax.broadcasted_iota(jnp.int32, (nrow, LANES), 1)

    def k_group(g, carry, next_src):
        gsum, pmax = carry
        nxt = g + ahead
        start_group(next_src, b, nxt if next_src is kc_ref else nxt - ngk, nxt)
        wait_group(kc_ref, g, g)
        for k in range(grp):
            j = g * grp + k
            kp = buf_ref[ring(g) + k].reshape(hd, PAGE_SIZE)
            s = _dot(qbd_b, kp.astype(BF16))
            sall_ref[j] = s
            gsum = gsum + jnp.where(gl == j // ppb, jnp.sum(s, -1, keepdims=True), 0.0)
            pmax = jnp.where(gl == j, jnp.max(s, -1, keepdims=True), pmax)
        return gsum, pmax

    carry = (jnp.zeros((nrow, LANES), F32), jnp.full((nrow, LANES), NEG_INF, F32))
    carry = lax.fori_loop(0, ngk - ahead, lambda g, c: k_group(g, c, kc_ref), carry)
    for g in range(ngk - ahead, ngk):
        carry = k_group(g, carry, vc_ref)
    gsum, pmax = carry

    gate = gsum * np.float32(1.0 / MOBA_BLOCK)
    gsel = jnp.where(gl < nfull, gate, NEG_INF)
    sel = jnp.zeros((nrow, LANES), F32)
    for _ in range(min(MOBA_TOPK, nfull)):
        m = jnp.max(gsel, -1, keepdims=True)
        idx = jnp.min(jnp.where(gsel == m, gl, LANES), -1, keepdims=True)
        hit = gl == idx
        sel = jnp.where(hit & (m > NEG_INF), 1.0, sel)
        gsel = jnp.where(hit, NEG_INF, gsel)

    kn = jnp.concatenate([kn_ref[...], jnp.zeros((LANES - tq, hd), F32)], axis=0).astype(BF16)
    vn = jnp.concatenate([vn_ref[...], jnp.zeros((LANES - tq, hd), F32)], axis=0).astype(BF16)
    s_new = _dot_nt(qbd_b, kn)
    nr = lax.broadcasted_iota(jnp.int32, (nrow, LANES), 0)
    s_new = jnp.where(gl <= (nr % tq), s_new, NEG_INF)

    eb_r = lax.broadcasted_iota(jnp.int32, (LANES, LANES), 0)
    eb_c = lax.broadcasted_iota(jnp.int32, (LANES, LANES), 1)
    expand = jnp.where((eb_c // ppb == eb_r) & (eb_c < npages), 1.0, 0.0).astype(BF16)
    selp = _dot(sel.astype(BF16), expand)
    m_all = jnp.maximum(jnp.max(jnp.where(selp > 0.0, pmax, NEG_INF), -1, keepdims=True),
                        jnp.max(s_new, -1, keepdims=True))
    p_new = jnp.exp(s_new - m_all)
    l_new = jnp.sum(p_new, -1, keepdims=True)
    acc0 = _dot(p_new.astype(BF16), vn)

    def shift_body(j, carry):
        col = jnp.max(jnp.where(gl == j, selp, 0.0), -1, keepdims=True)
        sall_ref[j] = jnp.where(col > 0.0, sall_ref[j] - m_all, NEG_INF)
        return carry

    lax.fori_loop(0, npages, shift_body, 0, unroll=4)

    def v_group(g, carry, prefetch):
        lsum, acc = carry
        if prefetch:
            start_group(vc_ref, b, g + ahead, ngk + g + ahead)
        else:
            @pl.when(b + 1 < nseq)
            def _():
                start_group(kc_ref, b + 1, g + ahead - ngk, ngk + g + ahead)
        wait_group(vc_ref, g, ngk + g)
        for k in range(grp):
            j = g * grp + k
            vp = buf_ref[ring(ngk + g) + k].reshape(hd, PAGE_SIZE).astype(BF16)
            p = jnp.exp(sall_ref[j])
            lsum = lsum + p
            acc = acc + _dot_nt(p.astype(BF16), vp)
        return lsum, acc

    carry = (jnp.zeros((nrow, LANES), F32), acc0)
    carry = lax.fori_loop(0, ngk - ahead, lambda g, c: v_group(g, c, True), carry)
    for g in range(ngk - ahead, ngk):
        carry = v_group(g, carry, False)
    lsum, acc = carry
    l_all = l_new + jnp.sum(lsum, -1, keepdims=True)
    res = jnp.where(diag, acc / l_all, 0.0).reshape(MOBA_HEADS, tq, hd)
    o_ref[...] = jnp.sum(res, axis=0)


def _moba_sample(h, page_table, cache_kt, cache_vt, layer, nseq, tq):
    npages = page_table.shape[1]
    hd = MOBA_W
    grp = _moba_group(npages)
    nslot = MOBA_RING * grp
    assert npages <= LANES and npages % (MOBA_BLOCK // PAGE_SIZE) == 0
    assert npages % grp == 0 and npages // grp >= MOBA_RING - 1
    grid_spec = pltpu.PrefetchScalarGridSpec(
        num_scalar_prefetch=1,
        grid=(nseq,),
        in_specs=[pl.BlockSpec((tq, hd), lambda b, pt: (b, COL_QB // 4)),
                  pl.BlockSpec((tq, hd), lambda b, pt: (b, COL_KB // 4)),
                  pl.BlockSpec((tq, hd), lambda b, pt: (b, COL_VB // 4)),
                  pl.BlockSpec(memory_space=pl.ANY),
                  pl.BlockSpec(memory_space=pl.ANY)],
        out_specs=pl.BlockSpec((tq, hd), lambda b, pt: (b, 0)),
        scratch_shapes=[pltpu.VMEM((nslot, MOBA_HEADS, MOBA_DH, PAGE_SIZE), F32),
                        pltpu.SemaphoreType.DMA((nslot,)),
                        pltpu.VMEM((npages, MOBA_HEADS * tq, PAGE_SIZE), F32)],
    )
    return pl.pallas_call(
        functools.partial(_moba_sample_kernel, layer=layer, npages=npages, tq=tq),
        grid_spec=grid_spec,
        out_shape=jax.ShapeDtypeStruct((nseq * tq, hd), F32),
        compiler_params=_cparams(("arbitrary",)),
        name="moba_sample",
    )(page_table, h, h, h, cache_kt, cache_vt)


def _sgu_kernel(u_ref, v_ref, g_ref, b_ref, w_ref, bias_ref, o_ref, *maybe_v_out, period):
    rows = u_ref.shape[0]
    u = _gelu_exact(u_ref[...])
    vn = _layer_norm(_gelu_exact(v_ref[...]), g_ref[...], b_ref[...])
    if maybe_v_out:
        maybe_v_out[0][...] = vn
    t = lax.broadcasted_iota(jnp.int32, (rows, rows), 0)
    s = lax.broadcasted_iota(jnp.int32, (rows, rows), 1)
    keep = ((t // period) == (s // period)) & ((s % period) <= (t % period))
    vb = vn.astype(BF16)
    mix = []
    for g in range(SG_GROUPS):
        wg = jnp.where(keep, w_ref[g], 0.0).astype(BF16)
        mix.append(_dot(wg, vb[:, g * SG_GC:(g + 1) * SG_GC]))
    o_ref[...] = u * (jnp.concatenate(mix, axis=1) + bias_ref[...])


def _sgu(h, ln_g, ln_b, wmix, bias, rows, period, want_v):
    n = h.shape[0]
    const = lambda shp: pl.BlockSpec(shp, lambda i: (0,) * len(shp))
    out_specs = [pl.BlockSpec((rows, SG_W), lambda i: (i, 0))]
    out_shape = [jax.ShapeDtypeStruct((n, SG_W), F32)]
    if want_v:
        out_specs.append(pl.BlockSpec((rows, SG_W), lambda i: (i, 0)))
        out_shape.append(jax.ShapeDtypeStruct((n, SG_W), F32))
    res = pl.pallas_call(
        functools.partial(_sgu_kernel, period=period),
        grid=(n // rows,),
        in_specs=[pl.BlockSpec((rows, SG_W), lambda i: (i, 6)),
                  pl.BlockSpec((rows, SG_W), lambda i: (i, 7)),
                  const((1, SG_W)), const((1, SG_W)),
                  const((SG_GROUPS, rows, rows)), const((rows, SG_W))],
        out_specs=out_specs,
        out_shape=out_shape,
        compiler_params=_cparams(("parallel",)),
        name="sgu",
    )(h, h, ln_g, ln_b, wmix, bias)
    return res if want_v else (res[0], None)


def _merge_kernel(oa_ref, ob_ref, oc_ref, ga_ref, gb_ref, gc_ref, x_ref, pa_ref, pb_ref, pc_ref, wo_ref,
                  g_ref, b_ref, o_ref):
    m = (jax.nn.sigmoid(ga_ref[...]) * _dot(oa_ref[...].astype(BF16), pa_ref[...])
         + jax.nn.sigmoid(gb_ref[...]) * _dot(ob_ref[...].astype(BF16), pb_ref[...])
         + jax.nn.sigmoid(gc_ref[...]) * _dot(oc_ref[...].astype(BF16), pc_ref[...]))
    y = np.float32(ALPHA) * x_ref[...] + _dot(m.astype(BF16), wo_ref[...])
    o_ref[...] = _layer_norm(y, g_ref[...], b_ref[...])


def _merge(oa, ob, oc, h, x, pa, pb, pc, wo, g, b):
    n = x.shape[0]
    tm = min(n, 512)
    row = lambda w, j: pl.BlockSpec((tm, w), lambda i: (i, j))
    const = lambda shp: pl.BlockSpec(shp, lambda i: (0,) * len(shp))
    return pl.pallas_call(
        _merge_kernel,
        grid=(n // tm,),
        in_specs=[row(GLA_V, 0), row(MOBA_W, 0), row(SG_W, 0),
                  row(D_MODEL, 4), row(D_MODEL, 5), row(D_MODEL, 6), row(D_MODEL, 0),
                  const((GLA_V, D_MODEL)), const((MOBA_W, D_MODEL)), const((SG_W, D_MODEL)),
                  const((D_MODEL, D_MODEL)), const((1, D_MODEL)), const((1, D_MODEL))],
        out_specs=row(D_MODEL, 0),
        out_shape=jax.ShapeDtypeStruct((n, D_MODEL), F32),
        compiler_params=_cparams(("parallel",)),
        name="merge",
    )(oa, ob, oc, h, h, h, x, pa, pb, pc, wo, g, b)


MOE_CHUNK_EXPERTS = 8


def _moe_kernel(x_ref, wr_ref, br_ref, wg_ref, wu_ref, wd_ref, sg_ref, su_ref, sd_ref, g_ref, b_ref,
                o_ref, xb_ref, gates_ref, acc_ref):
    c = pl.program_id(1)
    nchunk = pl.num_programs(1)
    tm = x_ref.shape[0]
    ce = MOE_CHUNK_EXPERTS
    gsz = N_EXPERTS // N_EXPERT_GROUPS

    @pl.when(c == 0)
    def _():
        x = x_ref[...]
        xb = x.astype(BF16)
        xb_ref[...] = xb
        s = jax.nn.sigmoid(_dot_nt(wr_ref[...], x, HIGHEST))
        sel = s + br_ref[...]
        sel3 = sel.reshape(N_EXPERT_GROUPS, gsz, tm)
        j3 = lax.broadcasted_iota(jnp.int32, sel3.shape, 1)
        m1 = jnp.max(sel3, axis=1, keepdims=True)
        i1 = jnp.min(jnp.where(sel3 == m1, j3, gsz), axis=1, keepdims=True)
        m2 = jnp.max(jnp.where(j3 == i1, NEG_INF, sel3), axis=1, keepdims=True)
        grp = (m1 + m2).reshape(N_EXPERT_GROUPS, tm)
        gi = lax.broadcasted_iota(jnp.int32, grp.shape, 0)
        gkeep = jnp.zeros(grp.shape, F32)
        for _ in range(TOPK_GROUPS):
            m = jnp.max(grp, axis=0, keepdims=True)
            idx = jnp.min(jnp.where(grp == m, gi, N_EXPERT_GROUPS), axis=0, keepdims=True)
            hit = gi == idx
            gkeep = jnp.where(hit, 1.0, gkeep)
            grp = jnp.where(hit, NEG_INF, grp)
        emask = jnp.broadcast_to(gkeep.reshape(N_EXPERT_GROUPS, 1, tm), sel3.shape).reshape(N_EXPERTS, tm)
        cand = jnp.where(emask > 0.0, sel, NEG_INF)
        ei = lax.broadcasted_iota(jnp.int32, cand.shape, 0)
        chosen = jnp.zeros(cand.shape, F32)
        for _ in range(TOP_K):
            m = jnp.max(cand, axis=0, keepdims=True)
            idx = jnp.min(jnp.where(cand == m, ei, N_EXPERTS), axis=0, keepdims=True)
            hit = ei == idx
            chosen = jnp.where(hit, 1.0, chosen)
            cand = jnp.where(hit, NEG_INF, cand)
        w = jnp.where(chosen > 0.0, s, 0.0)
        gates_ref[...] = (w / jnp.sum(w, axis=0, keepdims=True) * np.float32(ROUTED_SCALE)).reshape(
            N_EXPERTS // ce, ce, tm)
        hs = _silu(_dot(xb, sg_ref[...])) * _dot(xb, su_ref[...])
        acc_ref[...] = _dot(hs.astype(BF16), sd_ref[...])

    xb = xb_ref[...]
    hh = _silu(_dot(xb, wg_ref[...])) * _dot(xb, wu_ref[...])
    gt = jnp.concatenate([gates_ref[c], jnp.zeros((LANES - ce, tm), F32)], axis=0).T
    gexp = jnp.concatenate([jnp.broadcast_to(gt[:, j:j + 1], (tm, D_EXPERT)) for j in range(ce)], axis=1)
    acc_ref[...] += _dot((hh * gexp).astype(BF16), wd_ref[...])

    @pl.when(c == nchunk - 1)
    def _():
        o_ref[...] = _layer_norm(np.float32(ALPHA) * x_ref[...] + acc_ref[...], g_ref[...], b_ref[...])


def _moe(x, wr_t, br, wg, wu, wd, sg, su, sd, g, b):
    n = x.shape[0]
    tm = min(n, 1024)
    cw = MOE_CHUNK_EXPERTS * D_EXPERT
    nchunk = N_EXPERTS // MOE_CHUNK_EXPERTS
    const = lambda shp: pl.BlockSpec(shp, lambda i, c: (0,) * len(shp))
    return pl.pallas_call(
        _moe_kernel,
        grid=(n // tm, nchunk),
        in_specs=[pl.BlockSpec((tm, D_MODEL), lambda i, c: (i, 0)),
                  const((N_EXPERTS, D_MODEL)), const((N_EXPERTS, 1)),
                  pl.BlockSpec((D_MODEL, cw), lambda i, c: (0, c)),
                  pl.BlockSpec((D_MODEL, cw), lambda i, c: (0, c)),
                  pl.BlockSpec((cw, D_MODEL), lambda i, c: (c, 0)),
                  const((D_MODEL, D_SHARED)), const((D_MODEL, D_SHARED)), const((D_SHARED, D_MODEL)),
                  const((1, D_MODEL)), const((1, D_MODEL))],
        out_specs=pl.BlockSpec((tm, D_MODEL), lambda i, c: (i, 0)),
        out_shape=jax.ShapeDtypeStruct((n, D_MODEL), F32),
        scratch_shapes=[pltpu.VMEM((tm, D_MODEL), BF16),
                        pltpu.VMEM((nchunk, MOE_CHUNK_EXPERTS, tm), F32),
                        pltpu.VMEM((tm, D_MODEL), F32)],
        compiler_params=_cparams(("parallel", "arbitrary")),
        name="moe",
    )(x, wr_t, br, wg, wu, wd, sg, su, sd, g, b)


def _prep_weights(w_in, gla_w_a2, gla_b_a2, gla_norm_g, sg_ln_g, sg_ln_b, sg_w, sg_b, w_branch_a, w_branch_b,
                  w_branch_c, w_out, ln1_g, ln1_b, ln2_g, ln2_b, moe_w_router, moe_b_router, moe_w_gate,
                  moe_w_up, moe_w_down, sh_w_gate, sh_w_up, sh_w_down, dec_seq):
    depth = w_in.shape[0]
    c_alr = 2 * GLA_QK + 2 * GLA_V
    w_perm = jnp.concatenate(
        [w_in[:, :, :c_alr], w_in[:, :, c_alr + GLA_RANK:], w_in[:, :, c_alr:c_alr + GLA_RANK],
         jnp.zeros((depth, D_MODEL, LANES - GLA_RANK), w_in.dtype)], axis=2).astype(BF16)
    w2 = jnp.concatenate([gla_w_a2, jnp.zeros((depth, LANES - GLA_RANK, GLA_QK), F32)], axis=1).astype(BF16)
    reps = PAGE_SIZE // dec_seq if dec_seq < SG_CHUNK else 1
    nrow_s = 256
    tile_s = nrow_s // dec_seq
    return dict(
        w_perm=w_perm,
        w2=w2, w2t=jnp.swapaxes(w2, 1, 2),
        brow=gla_b_a2[:, None, :], bcol=gla_b_a2[:, :, None],
        gla_g=gla_norm_g[:, None, :],
        sg_g=sg_ln_g[:, None, :], sg_b=sg_ln_b[:, None, :],
        sg_w_p=sg_w,
        sg_bias_p=jnp.repeat(jnp.swapaxes(sg_b, 1, 2), SG_GC, axis=2),
        sg_w_s=jnp.tile(sg_w[:, :, :dec_seq, :dec_seq], (1, 1, tile_s, tile_s)),
        sg_bias_s=jnp.tile(jnp.repeat(jnp.swapaxes(sg_b[:, :, :dec_seq], 1, 2), SG_GC, axis=2), (1, tile_s, 1)),
        pa=w_branch_a.astype(BF16), pb=w_branch_b.astype(BF16), pc=w_branch_c.astype(BF16),
        wo=w_out.astype(BF16),
        ln1_g=ln1_g[:, None, :], ln1_b=ln1_b[:, None, :], ln2_g=ln2_g[:, None, :], ln2_b=ln2_b[:, None, :],
        wr_t=jnp.swapaxes(moe_w_router, 1, 2), br=moe_b_router[:, :, None],
        wg=jnp.transpose(moe_w_gate, (0, 2, 1, 3)).reshape(depth, D_MODEL, N_EXPERTS * D_EXPERT).astype(BF16),
        wu=jnp.transpose(moe_w_up, (0, 2, 1, 3)).reshape(depth, D_MODEL, N_EXPERTS * D_EXPERT).astype(BF16),
        wd=moe_w_down.reshape(depth, N_EXPERTS * D_EXPERT, D_MODEL).astype(BF16),
        sg=sh_w_gate.astype(BF16), su=sh_w_up.astype(BF16), sd=sh_w_down.astype(BF16),
    )


def _token_tail(x, h, oa, ob, oc, w, l):
    x1 = _merge(oa, ob, oc, h, x, w["pa"][l], w["pb"][l], w["pc"][l], w["wo"][l], w["ln1_g"][l], w["ln1_b"][l])
    return _moe(x1, w["wr_t"][l], w["br"][l], w["wg"][l], w["wu"][l], w["wd"][l], w["sg"][l], w["su"][l],
                w["sd"][l], w["ln2_g"][l], w["ln2_b"][l])


def kernel(x_prompt, x_sample, cache_k, cache_v, state_gla, page_table, w_in, gla_w_a2, gla_b_a2, gla_norm_g,
           sg_ln_g, sg_ln_b, sg_w, sg_b, w_branch_a, w_branch_b, w_branch_c, w_out, ln1_g, ln1_b, ln2_g, ln2_b,
           moe_w_router, moe_b_router, moe_w_gate, moe_w_up, moe_w_down, sh_w_gate, sh_w_up, sh_w_down):
    bp, sp, d = x_prompt.shape
    db, t, _ = x_sample.shape
    depth = w_in.shape[0]
    assert d == D_MODEL and sp % MOBA_BLOCK == 0 and sp % SG_CHUNK == 0 and t <= SUBLANES
    assert (db * t) % 256 == 0 or db * t == 256
    w = _prep_weights(w_in, gla_w_a2, gla_b_a2, gla_norm_g, sg_ln_g, sg_ln_b, sg_w, sg_b, w_branch_a,
                      w_branch_b, w_branch_c, w_out, ln1_g, ln1_b, ln2_g, ln2_b, moe_w_router, moe_b_router,
                      moe_w_gate, moe_w_up, moe_w_down, sh_w_gate, sh_w_up, sh_w_down, t)
    cache_kt = jnp.swapaxes(cache_k, 3, 4)
    cache_vt = jnp.swapaxes(cache_v, 3, 4)
    xp = x_prompt.reshape(bp * sp, d)
    xs = x_sample.reshape(db * t, d)
    zero_state = jnp.zeros((bp, GLA_HEADS, GLA_DK, GLA_DV), F32)
    gla_p, gla_s, kp_l, vp_l, ks_l, vs_l, sgu_l = [], [], [], [], [], [], []
    for l in range(depth):
        gla_args = (w["w2"][l], w["w2t"][l], w["brow"][l], w["bcol"][l], w["gla_g"][l])
        hp = _inproj(xp, w["w_perm"][l])
        oa, s_p = _gla(hp, zero_state, *gla_args, bp, sp, math.gcd(bp, GLA_SEQS_PER_STEP))
        ob, kt, vt = _moba_prompt(hp, bp, sp)
        oc, _ = _sgu(hp, w["sg_g"][l], w["sg_b"][l], w["sg_w_p"][l], w["sg_bias_p"][l], SG_CHUNK, SG_CHUNK, False)
        xp = _token_tail(xp, hp, oa, ob, oc, w, l)
        hs = _inproj(xs, w["w_perm"][l])
        oa, s_s = _gla(hs, state_gla[l], *gla_args, db, t, math.gcd(db, GLA_SEQS_PER_STEP))
        ob = _moba_sample(hs, page_table, cache_kt, cache_vt, l, db, t)
        oc, vcs = _sgu(hs, w["sg_g"][l], w["sg_b"][l], w["sg_w_s"][l], w["sg_bias_s"][l], db * t, t, True)
        xs = _token_tail(xs, hs, oa, ob, oc, w, l)
        gla_p.append(s_p)
        gla_s.append(s_s)
        kp_l.append(jnp.swapaxes(kt, 3, 4))
        vp_l.append(jnp.swapaxes(vt, 3, 4))
        kb_s = hs[:, COL_KB * LANES:COL_KB * LANES + MOBA_W].reshape(db, t, MOBA_HEADS, MOBA_DH)
        vb_s = hs[:, COL_VB * LANES:COL_VB * LANES + MOBA_W].reshape(db, t, MOBA_HEADS, MOBA_DH)
        ks_l.append(kb_s.transpose(0, 2, 1, 3))
        vs_l.append(vb_s.transpose(0, 2, 1, 3))
        sgu_l.append(vcs.reshape(db, t, SG_W))
    return (xp.reshape(bp, sp, d), xs.reshape(db, t, d), jnp.stack(gla_p), jnp.stack(gla_s),
            jnp.stack(kp_l), jnp.stack(vp_l), jnp.stack(ks_l), jnp.stack(vs_l), jnp.stack(sgu_l))
```

```python
import functools
import math

import numpy as np
import jax
import jax.numpy as jnp
from jax import lax
from jax.experimental import pallas as pl
from jax.experimental.pallas import tpu as pltpu

F32 = jnp.float32
BF16 = jnp.bfloat16

D_MODEL = 1024
DEPTH = 4
PAGE_SIZE = 128
GLA_HEADS, GLA_DK, GLA_DV, GLA_RANK, GLA_TAU = 4, 64, 128, 16, 16.0
GLA_QK, GLA_V = GLA_HEADS * GLA_DK, GLA_HEADS * GLA_DV
MOBA_HEADS, MOBA_DH, MOBA_BLOCK, MOBA_TOPK = 8, 64, 256, 3
MOBA_W = MOBA_HEADS * MOBA_DH
SG_GROUPS, SG_GC, SG_CHUNK = 4, 128, 128
SG_W = SG_GROUPS * SG_GC
N_EXPERTS, TOP_K, N_EXPERT_GROUPS, TOPK_GROUPS = 64, 8, 8, 4
D_EXPERT, D_SHARED, ROUTED_SCALE = 128, 128, 2.5
ALPHA = (2 * DEPTH) ** 0.25
LN_EPS = 1e-5

LANES = 128
SUBLANES = 8
VMEM_LIMIT_BYTES = 56 * 1024 * 1024

N_IN_PERM = 2 * GLA_QK + 2 * GLA_V + 3 * MOBA_W + 2 * SG_W + 3 * D_MODEL + LANES
COL_QB, COL_KB, COL_VB = 12, 16, 20
COL_ALR = 56
GLA_SUB = 16
GLA_SEQS_PER_STEP = 4
MOBA_PAIRS_PER_STEP = 4
SG_CHUNKS_PER_STEP = 4
NEG_INF = float("-inf")


def _row_tile(n, cap):
    t = cap
    while n % t:
        t //= 2
    return t


def _cparams(sem):
    return pltpu.CompilerParams(dimension_semantics=sem, vmem_limit_bytes=VMEM_LIMIT_BYTES)


def _layer_norm(x, g, b):
    mu = jnp.mean(x, -1, keepdims=True)
    xc = x - mu
    var = jnp.mean(xc * xc, -1, keepdims=True)
    return xc * lax.rsqrt(var + LN_EPS) * g + b


def _log_sigmoid(z):
    return jnp.minimum(z, 0.0) - jnp.log1p(jnp.exp(-jnp.abs(z)))


def _silu(x):
    return x * jax.nn.sigmoid(x)


def _gelu_exact(x):
    return 0.5 * x * (1.0 + lax.erf(x * np.float32(np.sqrt(0.5))))


def _dot(a, b, precision=None):
    return jnp.dot(a, b, preferred_element_type=F32, precision=precision)


def _dot_nt(a, b, precision=None):
    return lax.dot_general(a, b, (((1,), (1,)), ((), ())), preferred_element_type=F32, precision=precision)


def _dot_tn(a, b, precision=None):
    return lax.dot_general(a, b, (((0,), (0,)), ((), ())), preferred_element_type=F32, precision=precision)


HIGHEST = lax.Precision.HIGHEST


def _inproj_kernel(x_ref, w_ref, o_ref, xb_ref):
    @pl.when(pl.program_id(1) == 0)
    def _():
        xb_ref[...] = x_ref[...].astype(BF16)

    o_ref[...] = _dot(xb_ref[...], w_ref[...])


def _inproj(x, w):
    n, d = x.shape
    nc = w.shape[1]
    tm = _row_tile(n, 1024)
    tn = nc // 3
    return pl.pallas_call(
        _inproj_kernel,
        grid=(n // tm, nc // tn),
        in_specs=[pl.BlockSpec((tm, d), lambda i, j: (i, 0)),
                  pl.BlockSpec((d, tn), lambda i, j: (0, j))],
        out_specs=pl.BlockSpec((tm, tn), lambda i, j: (i, j)),
        out_shape=jax.ShapeDtypeStruct((n, nc), F32),
        scratch_shapes=[pltpu.VMEM((tm, d), BF16)],
        compiler_params=_cparams(("parallel", "arbitrary")),
        name="inproj",
    )(x, w)


def _gla_kernel(q_ref, k_ref, v_ref, r_ref, alr_ref, w2_ref, w2t_ref, brow_ref, bcol_ref, g_ref, s0_ref,
                o_ref, sfin_ref, s_ref, oraw_ref, *, nb, rows, pad):
    c = pl.program_id(1)
    nsteps = pl.num_programs(1)
    rr = rows + pad
    nblk = rr // GLA_SUB

    @pl.when(c == 0)
    def _():
        s_ref[...] = s0_ref[...]

    def padrows(a):
        if pad == 0:
            return a
        return jnp.concatenate([a, jnp.zeros((pad, a.shape[1]), a.dtype)], axis=0)

    ti = lax.broadcasted_iota(jnp.int32, (rr, rr), 0)
    si = lax.broadcasted_iota(jnp.int32, (rr, rr), 1)
    same = (ti // GLA_SUB) == (si // GLA_SUB)
    m_incl = jnp.where(same & (si <= ti), 1.0, 0.0).astype(F32)
    m_blk = jnp.where(same, 1.0, 0.0).astype(F32)
    bi_r = lax.broadcasted_iota(jnp.int32, (rr, LANES), 0)
    bi_c = lax.broadcasted_iota(jnp.int32, (rr, LANES), 1)
    m_ind = jnp.where((bi_r // GLA_SUB) == bi_c, 1.0, 0.0).astype(F32)
    he_r = lax.broadcasted_iota(jnp.int32, (GLA_QK, GLA_V), 0)
    he_c = lax.broadcasted_iota(jnp.int32, (GLA_QK, GLA_V), 1)
    he = jnp.where((he_r // GLA_DK) == (he_c // GLA_DV), 1.0, 0.0).astype(BF16)
    trow = lax.broadcasted_iota(jnp.int32, (GLA_SUB, 1), 0)

    seqs = []
    for b in range(nb):
        q = padrows(q_ref[b]) * np.float32(GLA_DK ** -0.5)
        k = padrows(k_ref[b])
        v = padrows(v_ref[b])
        alr = padrows(alr_ref[b]).astype(BF16)
        la = _log_sigmoid(_dot(alr, w2_ref[...]) + brow_ref[...]) * np.float32(1.0 / GLA_TAU)
        lat = _log_sigmoid(_dot_nt(w2t_ref[...], alr) + bcol_ref[...]) * np.float32(1.0 / GLA_TAU)
        if pad:
            la = jnp.where(lax.broadcasted_iota(jnp.int32, la.shape, 0) < rows, la, 0.0)
            lat = jnp.where(lax.broadcasted_iota(jnp.int32, lat.shape, 1) < rows, lat, 0.0)
        b_loc = _dot(m_incl, la, HIGHEST)
        b_tot = _dot(m_blk, la, HIGHEST)
        seqs.append(dict(
            q=q, k=k, v=v, b_loc=b_loc,
            qd=(q * jnp.exp(b_loc)).astype(BF16),
            kd=(k * jnp.exp(b_tot - b_loc)).astype(BF16),
            vb=v.astype(BF16),
            dcol=jnp.exp(_dot(lat, m_ind, HIGHEST)),
            s=[s_ref[b, h] for h in range(GLA_HEADS)]))

    for i in range(nblk):
        r0 = i * GLA_SUB
        for b in range(nb):
            sq = seqs[b]
            b_i = sq["b_loc"][r0:r0 + GLA_SUB]
            q_i = sq["q"][r0:r0 + GLA_SUB]
            k_i = sq["k"][r0:r0 + GLA_SUB]
            v_i = sq["v"][r0:r0 + GLA_SUB]
            half = GLA_SUB // 2
            parts = []
            for s in range(GLA_SUB):
                t0 = 0 if s < half else half
                e = jnp.exp(jnp.minimum(b_i[t0:] - b_i[s:s + 1], 0.0))
                parts.append(jnp.where(trow[t0:] >= s, q_i[t0:] * e * k_i[s:s + 1], 0.0))
            pstack = jnp.concatenate(parts, axis=0).astype(BF16)
            rexp = _dot(pstack, he)
            o_blk = rexp[0:GLA_SUB] * v_i[0:1]
            for s in range(1, half):
                o_blk = o_blk + rexp[s * GLA_SUB:(s + 1) * GLA_SUB] * v_i[s:s + 1]
            base = half * GLA_SUB
            o_hi = rexp[base:base + half] * v_i[half:half + 1]
            for s in range(half + 1, GLA_SUB):
                r1 = base + (s - half) * half
                o_hi = o_hi + rexp[r1:r1 + half] * v_i[s:s + 1]
            o_blk = o_blk + jnp.concatenate([jnp.zeros((half, GLA_V), F32), o_hi], axis=0)
            outs = []
            for h in range(GLA_HEADS):
                ks = slice(h * GLA_DK, (h + 1) * GLA_DK)
                vs = slice(h * GLA_DV, (h + 1) * GLA_DV)
                s_h = sq["s"][h]
                outs.append(_dot(sq["qd"][r0:r0 + GLA_SUB, ks], s_h.astype(BF16)))
                upd = _dot_tn(sq["kd"][r0:r0 + GLA_SUB, ks], sq["vb"][r0:r0 + GLA_SUB, vs])
                sq["s"][h] = sq["dcol"][ks, i:i + 1] * s_h + upd
            oraw_ref[b, r0:r0 + GLA_SUB, :] = o_blk + jnp.concatenate(outs, axis=1)

    g = g_ref[...]
    for b in range(nb):
        for h in range(GLA_HEADS):
            s_ref[b, h] = seqs[b]["s"][h]
        o = oraw_ref[b, 0:rows, :]
        normed = []
        for h in range(GLA_HEADS):
            oh = o[:, h * GLA_DV:(h + 1) * GLA_DV]
            ms = jnp.mean(oh * oh, -1, keepdims=True)
            normed.append(oh * lax.rsqrt(ms + 1e-6) * g)
        o_ref[b] = jnp.concatenate(normed, axis=1) * _silu(r_ref[b])

    @pl.when(c == nsteps - 1)
    def _():
        sfin_ref[...] = s_ref[...]


def _gla(h, s0, w2, w2t, brow, bcol, g, nseq, seqlen, nb):
    rows = min(seqlen, 128)
    pad = (-rows) % GLA_SUB
    steps = seqlen // rows
    h3 = h.reshape(nseq, seqlen, h.shape[1])
    blk = lambda w, j: pl.BlockSpec((nb, rows, w), lambda b, c: (b, c, j))
    const = lambda shp: pl.BlockSpec(shp, lambda b, c: (0,) * len(shp))
    state = pl.BlockSpec((nb, GLA_HEADS, GLA_DK, GLA_DV), lambda b, c: (b, 0, 0, 0))
    o, sfin = pl.pallas_call(
        functools.partial(_gla_kernel, nb=nb, rows=rows, pad=pad),
        grid=(nseq // nb, steps),
        in_specs=[blk(GLA_QK, 0), blk(GLA_QK, 1), blk(GLA_V, 1), blk(GLA_V, 2), blk(LANES, COL_ALR),
                  const((LANES, GLA_QK)), const((GLA_QK, LANES)), const((1, GLA_QK)), const((GLA_QK, 1)),
                  const((1, GLA_DV)), state],
        out_specs=[pl.BlockSpec((nb, rows, GLA_V), lambda b, c: (b, c, 0)), state],
        out_shape=[jax.ShapeDtypeStruct((nseq, seqlen, GLA_V), F32),
                   jax.ShapeDtypeStruct((nseq, GLA_HEADS, GLA_DK, GLA_DV), F32)],
        scratch_shapes=[pltpu.VMEM((nb, GLA_HEADS, GLA_DK, GLA_DV), F32),
                        pltpu.VMEM((nb, rows + pad, GLA_V), F32)],
        compiler_params=_cparams(("parallel", "arbitrary")),
        name="gla",
    )(h3, h3, h3, h3, h3, w2, w2t, brow, bcol, g, s0)
    return o.reshape(nseq * seqlen, GLA_V), sfin


def _moba_prompt_kernel(q_ref, k_ref, v_ref, o_ref, kt_ref, vt_ref, kb_ref, vtb_ref, km_ref, sel_ref, s_ref,
                        acc_ref, *, nblk, npg):
    i = pl.program_id(2)
    blk = MOBA_BLOCK
    nh = 2 * npg

    @pl.when(i == 0)
    def _():
        km_ref[...] = jnp.zeros(km_ref.shape, F32)
        kb_ref[...] = jnp.zeros(kb_ref.shape, BF16)
        vtb_ref[...] = jnp.zeros(vtb_ref.shape, BF16)

    kf = k_ref[...]
    kt = kf.T
    vt = v_ref[...].T
    k_own = kf.astype(BF16)
    vt_own = vt.astype(BF16)
    kb_ref[i] = k_own
    vtb_ref[i] = vt_own
    for pg in range(blk // PAGE_SIZE):
        kt_ref[0, pg] = kt[:, pg * PAGE_SIZE:(pg + 1) * PAGE_SIZE].reshape(nh, MOBA_DH, PAGE_SIZE)
        vt_ref[0, pg] = vt[:, pg * PAGE_SIZE:(pg + 1) * PAGE_SIZE].reshape(nh, MOBA_DH, PAGE_SIZE)

    qt = (q_ref[...] * np.float32(MOBA_DH ** -0.5)).T
    drow = lax.broadcasted_iota(jnp.int32, (LANES, blk), 0)
    brow = lax.broadcasted_iota(jnp.int32, (nblk, blk), 0)
    krow = lax.broadcasted_iota(jnp.int32, (blk, blk), 0)
    qcol = lax.broadcasted_iota(jnp.int32, (blk, blk), 1)
    km = km_ref[...]
    ones = jnp.ones((2 * SUBLANES, blk), BF16)

    def slab(a, hh):
        return a[:, (hh // 2) * LANES:(hh // 2 + 1) * LANES]

    def lhs_rows(vtj, hh):
        return jnp.concatenate([vtj[hh * MOBA_DH:(hh + 1) * MOBA_DH], ones], axis=0)

    qbs, ms = [], []
    for hh in range(nh):
        qm = jnp.where((drow // MOBA_DH) == (hh % 2), qt[(hh // 2) * LANES:(hh // 2 + 1) * LANES], 0.0)
        gate = _dot(slab(km, hh), qm, HIGHEST)
        gsel = jnp.where(brow < i, gate, NEG_INF)
        sel = jnp.zeros((nblk, blk), F32)
        for _ in range(MOBA_TOPK):
            m = jnp.max(gsel, axis=0, keepdims=True)
            idx = jnp.min(jnp.where(gsel == m, brow, nblk), axis=0, keepdims=True)
            hit = brow == idx
            sel = jnp.where(hit & (m > NEG_INF), 1.0, sel)
            gsel = jnp.where(hit, NEG_INF, gsel)
        sel_ref[hh] = sel
        qb = qm.astype(BF16)
        s = jnp.where(krow <= qcol, _dot(slab(k_own, hh), qb), NEG_INF)
        m0 = jnp.max(s, axis=0, keepdims=True)
        p = jnp.exp((s - m0).astype(BF16))
        acc_ref[hh] = _dot(lhs_rows(vt_own, hh), p)
        ms.append(m0)
        qbs.append(qb)

    km_ref[pl.ds(i, 1), :] = jnp.sum(kf, axis=0, keepdims=True) * np.float32(1.0 / blk)

    npairs = (i + 1) // 2

    def stage_a(t, slot):
        mx = [jnp.full((1, blk), NEG_INF, F32) for _ in range(nh)]
        for bb in range(2):
            j = jnp.minimum(2 * t + bb, nblk - 1)
            kj = kb_ref[j]
            for hh in range(nh):
                s = jnp.where(sel_ref[hh, pl.ds(j, 1), :] > 0.0, _dot(slab(kj, hh), qbs[hh]), NEG_INF)
                s_ref[hh, slot, bb] = s
                mx[hh] = jnp.maximum(mx[hh], jnp.max(s, axis=0, keepdims=True))
        return mx

    def stage_b(t, slot, ms, mx):
        ms = list(ms)
        vts = [vtb_ref[jnp.minimum(2 * t + bb, nblk - 1)] for bb in range(2)]
        for hh in range(nh):
            m_new = jnp.maximum(ms[hh], mx[hh])
            contrib = None
            for bb in range(2):
                p = jnp.exp((s_ref[hh, slot, bb] - m_new).astype(BF16))
                d = _dot(lhs_rows(vts[bb], hh), p)
                contrib = d if contrib is None else contrib + d
            acc_ref[hh] = jnp.exp(ms[hh] - m_new) * acc_ref[hh] + contrib
            ms[hh] = m_new
        return ms

    def body(u, c):
        ms, mx0 = c[:nh], c[nh:]
        mx1 = stage_a(2 * u + 1, 1)
        ms = stage_b(2 * u, 0, ms, mx0)
        mx0 = stage_a(2 * u + 2, 0)
        ms = stage_b(2 * u + 1, 1, ms, mx1)
        return (*ms, *mx0)

    nloop = jnp.maximum((npairs - 1) // 2, 0)
    c = lax.fori_loop(0, nloop, body, (*ms, *stage_a(0, 0)))
    mx1 = stage_a(2 * nloop + 1, 1)
    ms = stage_b(2 * nloop, 0, c[:nh], c[nh:])
    stage_b(2 * nloop + 1, 1, ms, mx1)
    outs = []
    for hh in range(nh):
        a = acc_ref[hh]
        outs.append(a[:MOBA_DH] / a[MOBA_DH:MOBA_DH + 1])
    o_ref[...] = jnp.concatenate(outs, axis=0).T


def _moba_prompt(h, nseq, seqlen):
    nblk = seqlen // MOBA_BLOCK
    npg = MOBA_PAIRS_PER_STEP
    ngrp = MOBA_HEADS // 2 // npg
    nh = 2 * npg
    w = npg * LANES
    ppb = MOBA_BLOCK // PAGE_SIZE
    cq, ck, cv = COL_QB // npg, COL_KB // npg, COL_VB // npg
    o, kt, vt = pl.pallas_call(
        functools.partial(_moba_prompt_kernel, nblk=nblk, npg=npg),
        grid=(nseq, ngrp, nblk),
        in_specs=[pl.BlockSpec((MOBA_BLOCK, w), lambda b, p, i: (b * nblk + i, cq + p)),
                  pl.BlockSpec((MOBA_BLOCK, w), lambda b, p, i: (b * nblk + i, ck + p)),
                  pl.BlockSpec((MOBA_BLOCK, w), lambda b, p, i: (b * nblk + i, cv + p))],
        out_specs=[pl.BlockSpec((MOBA_BLOCK, w), lambda b, p, i: (b * nblk + i, p)),
                   pl.BlockSpec((1, ppb, nh, MOBA_DH, PAGE_SIZE), lambda b, p, i: (b, i, p, 0, 0)),
                   pl.BlockSpec((1, ppb, nh, MOBA_DH, PAGE_SIZE), lambda b, p, i: (b, i, p, 0, 0))],
        out_shape=[jax.ShapeDtypeStruct((nseq * seqlen, MOBA_W), F32),
                   jax.ShapeDtypeStruct((nseq, seqlen // PAGE_SIZE, MOBA_HEADS, MOBA_DH, PAGE_SIZE), F32),
                   jax.ShapeDtypeStruct((nseq, seqlen // PAGE_SIZE, MOBA_HEADS, MOBA_DH, PAGE_SIZE), F32)],
        scratch_shapes=[pltpu.VMEM((nblk, MOBA_BLOCK, w), BF16),
                        pltpu.VMEM((nblk, w, MOBA_BLOCK), BF16),
                        pltpu.VMEM((nblk, w), F32),
                        pltpu.VMEM((nh, nblk, MOBA_BLOCK), F32),
                        pltpu.VMEM((nh, 2, 2, MOBA_BLOCK, MOBA_BLOCK), F32),
                        pltpu.VMEM((nh, MOBA_DH + 2 * SUBLANES, MOBA_BLOCK), F32)],
        compiler_params=_cparams(("parallel", "parallel", "arbitrary")),
        name="moba_prompt",
    )(h, h, h)
    return o, kt, vt


MOBA_RING = 3
MOBA_GROUP_MAX = 8


def _moba_group(npages):
    return min(MOBA_GROUP_MAX, npages // 2)


def _moba_sample_kernel(pt_ref, q_ref, kn_ref, vn_ref, kc_ref, vc_ref, o_ref,
                        buf_ref, sem_ref, sall_ref, *, layer, npages, tq):
    b = pl.program_id(0)
    hd = MOBA_HEADS * MOBA_DH
    nrow = MOBA_HEADS * tq
    ppb = MOBA_BLOCK // PAGE_SIZE
    nfull = npages // ppb
    grp = _moba_group(npages)
    ngk = npages // grp
    ahead = MOBA_RING - 1

    nseq = pl.num_programs(0)
    base = b * (2 * ngk)

    def ring(g_stream):
        return ((base + g_stream) % MOBA_RING) * grp

    def copy(src_ref, seq, page, slot):
        return pltpu.make_async_copy(src_ref.at[layer, pt_ref[seq, page]], buf_ref.at[slot], sem_ref.at[slot])

    def start_group(src_ref, seq, g_local, g_stream):
        for k in range(grp):
            copy(src_ref, seq, g_local * grp + k, ring(g_stream) + k).start()

    def wait_group(src_ref, g_local, g_stream):
        for k in range(grp):
            copy(src_ref, b, g_local * grp + k, ring(g_stream) + k).wait()

    @pl.when(b == 0)
    def _():
        for g in range(ahead):
            start_group(kc_ref, b, g, g)

    qs = q_ref[...] * np.float32(MOBA_DH ** -0.5)
    qt = jnp.concatenate([qs] * MOBA_HEADS, axis=0)
    r_i = lax.broadcasted_iota(jnp.int32, (nrow, hd), 0)
    c_i = lax.broadcasted_iota(jnp.int32, (nrow, hd), 1)
    diag = (r_i // tq) == (c_i // MOBA_DH)
    qbd = jnp.where(diag, qt, 0.0)
    qbd_b = qbd.astype(BF16)
    gl = lax.broadcasted_iota(jnp.int32, (nrow, LANES), 1)

    def k_group(g, carry, next_src):
        gsum, pmax = carry
        nxt = g + ahead
        start_group(next_src, b, nxt if next_src is kc_ref else nxt - ngk, nxt)
        wait_group(kc_ref, g, g)
        for k in range(grp):
            j = g * grp + k
            kp = buf_ref[ring(g) + k].reshape(hd, PAGE_SIZE)
            s = _dot(qbd_b, kp.astype(BF16))
            sall_ref[j] = s
            gsum = gsum + jnp.where(gl == j // ppb, jnp.sum(s, -1, keepdims=True), 0.0)
            pmax = jnp.where(gl == j, jnp.max(s, -1, keepdims=True), pmax)
        return gsum, pmax

    carry = (jnp.zeros((nrow, LANES), F32), jnp.full((nrow, LANES), NEG_INF, F32))
    carry = lax.fori_loop(0, ngk - ahead, lambda g, c: k_group(g, c, kc_ref), carry)
    for g in range(ngk - ahead, ngk):
        carry = k_group(g, carry, vc_ref)
    gsum, pmax = carry

    gate = gsum * np.float32(1.0 / MOBA_BLOCK)
    gsel = jnp.where(gl < nfull, gate, NEG_INF)
    sel = jnp.zeros((nrow, LANES), F32)
    for _ in range(min(MOBA_TOPK, nfull)):
        m = jnp.max(gsel, -1, keepdims=True)
        idx = jnp.min(jnp.where(gsel == m, gl, LANES), -1, keepdims=True)
        hit = gl == idx
        sel = jnp.where(hit & (m > NEG_INF), 1.0, sel)
        gsel = jnp.where(hit, NEG_INF, gsel)

    kn = jnp.concatenate([kn_ref[...], jnp.zeros((LANES - tq, hd), F32)], axis=0).astype(BF16)
    vn = jnp.concatenate([vn_ref[...], jnp.zeros((LANES - tq, hd), F32)], axis=0).astype(BF16)
    s_new = _dot_nt(qbd_b, kn)
    nr = lax.broadcasted_iota(jnp.int32, (nrow, LANES), 0)
    s_new = jnp.where(gl <= (nr % tq), s_new, NEG_INF)

    eb_r = lax.broadcasted_iota(jnp.int32, (LANES, LANES), 0)
    eb_c = lax.broadcasted_iota(jnp.int32, (LANES, LANES), 1)
    expand = jnp.where((eb_c // ppb == eb_r) & (eb_c < npages), 1.0, 0.0).astype(BF16)
    selp = _dot(sel.astype(BF16), expand)
    m_all = jnp.maximum(jnp.max(jnp.where(selp > 0.0, pmax, NEG_INF), -1, keepdims=True),
                        jnp.max(s_new, -1, keepdims=True))
    p_new = jnp.exp(s_new - m_all)
    l_new = jnp.sum(p_new, -1, keepdims=True)
    acc0 = _dot(p_new.astype(BF16), vn)

    def shift_body(j, carry):
        col = jnp.max(jnp.where(gl == j, selp, 0.0), -1, keepdims=True)
        sall_ref[j] = jnp.where(col > 0.0, sall_ref[j] - m_all, NEG_INF)
        return carry

    lax.fori_loop(0, npages, shift_body, 0, unroll=4)

    def v_group(g, carry, prefetch):
        lsum, acc = carry
        if prefetch:
            start_group(vc_ref, b, g + ahead, ngk + g + ahead)
        else:
            @pl.when(b + 1 < nseq)
            def _():
                start_group(kc_ref, b + 1, g + ahead - ngk, ngk + g + ahead)
        wait_group(vc_ref, g, ngk + g)
        for k in range(grp):
            j = g * grp + k
            vp = buf_ref[ring(ngk + g) + k].reshape(hd, PAGE_SIZE).astype(BF16)
            p = jnp.exp(sall_ref[j])
            lsum = lsum + p
            acc = acc + _dot_nt(p.astype(BF16), vp)
        return lsum, acc

    carry = (jnp.zeros((nrow, LANES), F32), acc0)
    carry = lax.fori_loop(0, ngk - ahead, lambda g, c: v_group(g, c, True), carry)
    for g in range(ngk - ahead, ngk):
        carry = v_group(g, carry, False)
    lsum, acc = carry
    l_all = l_new + jnp.sum(lsum, -1, keepdims=True)
    res = jnp.where(diag, acc / l_all, 0.0).reshape(MOBA_HEADS, tq, hd)
    o_ref[...] = jnp.sum(res, axis=0)


def _moba_sample(h, page_table, cache_kt, cache_vt, layer, nseq, tq):
    npages = page_table.shape[1]
    hd = MOBA_W
    grp = _moba_group(npages)
    nslot = MOBA_RING * grp
    assert npages <= LANES and npages % (MOBA_BLOCK // PAGE_SIZE) == 0
    assert npages % grp == 0 and npages // grp >= MOBA_RING - 1
    grid_spec = pltpu.PrefetchScalarGridSpec(
        num_scalar_prefetch=1,
        grid=(nseq,),
        in_specs=[pl.BlockSpec((tq, hd), lambda b, pt: (b, COL_QB // 4)),
                  pl.BlockSpec((tq, hd), lambda b, pt: (b, COL_KB // 4)),
                  pl.BlockSpec((tq, hd), lambda b, pt: (b, COL_VB // 4)),
                  pl.BlockSpec(memory_space=pl.ANY),
                  pl.BlockSpec(memory_space=pl.ANY)],
        out_specs=pl.BlockSpec((tq, hd), lambda b, pt: (b, 0)),
        scratch_shapes=[pltpu.VMEM((nslot, MOBA_HEADS, MOBA_DH, PAGE_SIZE), F32),
                        pltpu.SemaphoreType.DMA((nslot,)),
                        pltpu.VMEM((npages, MOBA_HEADS * tq, PAGE_SIZE), F32)],
    )
    return pl.pallas_call(
        functools.partial(_moba_sample_kernel, layer=layer, npages=npages, tq=tq),
        grid_spec=grid_spec,
        out_shape=jax.ShapeDtypeStruct((nseq * tq, hd), F32),
        compiler_params=_cparams(("arbitrary",)),
        name="moba_sample",
    )(page_table, h, h, h, cache_kt, cache_vt)


def _sgu_kernel(u_ref, v_ref, g_ref, b_ref, w_ref, bias_ref, o_ref, *maybe_v_out, period):
    rows = u_ref.shape[0]
    u = _gelu_exact(u_ref[...])
    vn = _layer_norm(_gelu_exact(v_ref[...]), g_ref[...], b_ref[...])
    if maybe_v_out:
        maybe_v_out[0][...] = vn
    t = lax.broadcasted_iota(jnp.int32, (rows, rows), 0)
    s = lax.broadcasted_iota(jnp.int32, (rows, rows), 1)
    keep = ((t // period) == (s // period)) & ((s % period) <= (t % period))
    vb = vn.astype(BF16)
    mix = []
    for g in range(SG_GROUPS):
        wg = jnp.where(keep, w_ref[g], 0.0).astype(BF16)
        mix.append(_dot(wg, vb[:, g * SG_GC:(g + 1) * SG_GC]))
    o_ref[...] = u * (jnp.concatenate(mix, axis=1) + bias_ref[...])


def _sgu(h, ln_g, ln_b, wmix, bias, rows, period, want_v):
    n = h.shape[0]
    const = lambda shp: pl.BlockSpec(shp, lambda i: (0,) * len(shp))
    out_specs = [pl.BlockSpec((rows, SG_W), lambda i: (i, 0))]
    out_shape = [jax.ShapeDtypeStruct((n, SG_W), F32)]
    if want_v:
        out_specs.append(pl.BlockSpec((rows, SG_W), lambda i: (i, 0)))
        out_shape.append(jax.ShapeDtypeStruct((n, SG_W), F32))
    res = pl.pallas_call(
        functools.partial(_sgu_kernel, period=period),
        grid=(n // rows,),
        in_specs=[pl.BlockSpec((rows, SG_W), lambda i: (i, 6)),
                  pl.BlockSpec((rows, SG_W), lambda i: (i, 7)),
                  const((1, SG_W)), const((1, SG_W)),
                  const((SG_GROUPS, rows, rows)), const((rows, SG_W))],
        out_specs=out_specs,
        out_shape=out_shape,
        compiler_params=_cparams(("parallel",)),
        name="sgu",
    )(h, h, ln_g, ln_b, wmix, bias)
    return res if want_v else (res[0], None)


def _merge_kernel(oa_ref, ob_ref, oc_ref, ga_ref, gb_ref, gc_ref, x_ref, pa_ref, pb_ref, pc_ref, wo_ref,
                  g_ref, b_ref, o_ref):
    m = (jax.nn.sigmoid(ga_ref[...]) * _dot(oa_ref[...].astype(BF16), pa_ref[...])
         + jax.nn.sigmoid(gb_ref[...]) * _dot(ob_ref[...].astype(BF16), pb_ref[...])
         + jax.nn.sigmoid(gc_ref[...]) * _dot(oc_ref[...].astype(BF16), pc_ref[...]))
    y = np.float32(ALPHA) * x_ref[...] + _dot(m.astype(BF16), wo_ref[...])
    o_ref[...] = _layer_norm(y, g_ref[...], b_ref[...])


def _merge(oa, ob, oc, h, x, pa, pb, pc, wo, g, b):
    n = x.shape[0]
    tm = _row_tile(n, 512)
    row = lambda w, j: pl.BlockSpec((tm, w), lambda i: (i, j))
    const = lambda shp: pl.BlockSpec(shp, lambda i: (0,) * len(shp))
    return pl.pallas_call(
        _merge_kernel,
        grid=(n // tm,),
        in_specs=[row(GLA_V, 0), row(MOBA_W, 0), row(SG_W, 0),
                  row(D_MODEL, 4), row(D_MODEL, 5), row(D_MODEL, 6), row(D_MODEL, 0),
                  const((GLA_V, D_MODEL)), const((MOBA_W, D_MODEL)), const((SG_W, D_MODEL)),
                  const((D_MODEL, D_MODEL)), const((1, D_MODEL)), const((1, D_MODEL))],
        out_specs=row(D_MODEL, 0),
        out_shape=jax.ShapeDtypeStruct((n, D_MODEL), F32),
        compiler_params=_cparams(("parallel",)),
        name="merge",
    )(oa, ob, oc, h, h, h, x, pa, pb, pc, wo, g, b)


MOE_CHUNK_EXPERTS = 8


def _moe_kernel(x_ref, wr_ref, br_ref, wg_ref, wu_ref, wd_ref, sg_ref, su_ref, sd_ref, g_ref, b_ref,
                o_ref, xb_ref, gates_ref, acc_ref):
    c = pl.program_id(1)
    nchunk = pl.num_programs(1)
    tm = x_ref.shape[0]
    ce = MOE_CHUNK_EXPERTS
    gsz = N_EXPERTS // N_EXPERT_GROUPS

    @pl.when(c == 0)
    def _():
        x = x_ref[...]
        xb = x.astype(BF16)
        xb_ref[...] = xb
        s = jax.nn.sigmoid(_dot_nt(wr_ref[...], x, HIGHEST))
        sel = s + br_ref[...]
        sel3 = sel.reshape(N_EXPERT_GROUPS, gsz, tm)
        j3 = lax.broadcasted_iota(jnp.int32, sel3.shape, 1)
        m1 = jnp.max(sel3, axis=1, keepdims=True)
        i1 = jnp.min(jnp.where(sel3 == m1, j3, gsz), axis=1, keepdims=True)
        m2 = jnp.max(jnp.where(j3 == i1, NEG_INF, sel3), axis=1, keepdims=True)
        grp = (m1 + m2).reshape(N_EXPERT_GROUPS, tm)
        gi = lax.broadcasted_iota(jnp.int32, grp.shape, 0)
        gkeep = jnp.zeros(grp.shape, F32)
        for _ in range(TOPK_GROUPS):
            m = jnp.max(grp, axis=0, keepdims=True)
            idx = jnp.min(jnp.where(grp == m, gi, N_EXPERT_GROUPS), axis=0, keepdims=True)
            hit = gi == idx
            gkeep = jnp.where(hit, 1.0, gkeep)
            grp = jnp.where(hit, NEG_INF, grp)
        emask = jnp.broadcast_to(gkeep.reshape(N_EXPERT_GROUPS, 1, tm), sel3.shape).reshape(N_EXPERTS, tm)
        cand = jnp.where(emask > 0.0, sel, NEG_INF)
        ei = lax.broadcasted_iota(jnp.int32, cand.shape, 0)
        chosen = jnp.zeros(cand.shape, F32)
        for _ in range(TOP_K):
            m = jnp.max(cand, axis=0, keepdims=True)
            idx = jnp.min(jnp.where(cand == m, ei, N_EXPERTS), axis=0, keepdims=True)
            hit = ei == idx
            chosen = jnp.where(hit, 1.0, chosen)
            cand = jnp.where(hit, NEG_INF, cand)
        w = jnp.where(chosen > 0.0, s, 0.0)
        gates_ref[...] = (w / jnp.sum(w, axis=0, keepdims=True) * np.float32(ROUTED_SCALE)).reshape(
            N_EXPERTS // ce, ce, tm)
        hs = _silu(_dot(xb, sg_ref[...])) * _dot(xb, su_ref[...])
        acc_ref[...] = _dot(hs.astype(BF16), sd_ref[...])

    xb = xb_ref[...]
    hh = _silu(_dot(xb, wg_ref[...])) * _dot(xb, wu_ref[...])
    gt = jnp.concatenate([gates_ref[c], jnp.zeros((LANES - ce, tm), F32)], axis=0).T
    gexp = jnp.concatenate([jnp.broadcast_to(gt[:, j:j + 1], (tm, D_EXPERT)) for j in range(ce)], axis=1)
    acc_ref[...] += _dot((hh * gexp).astype(BF16), wd_ref[...])

    @pl.when(c == nchunk - 1)
    def _():
        o_ref[...] = _layer_norm(np.float32(ALPHA) * x_ref[...] + acc_ref[...], g_ref[...], b_ref[...])


def _moe(x, wr_t, br, wg, wu, wd, sg, su, sd, g, b):
    n = x.shape[0]
    tm = _row_tile(n, 1024)
    cw = MOE_CHUNK_EXPERTS * D_EXPERT
    nchunk = N_EXPERTS // MOE_CHUNK_EXPERTS
    const = lambda shp: pl.BlockSpec(shp, lambda i, c: (0,) * len(shp))
    return pl.pallas_call(
        _moe_kernel,
        grid=(n // tm, nchunk),
        in_specs=[pl.BlockSpec((tm, D_MODEL), lambda i, c: (i, 0)),
                  const((N_EXPERTS, D_MODEL)), const((N_EXPERTS, 1)),
                  pl.BlockSpec((D_MODEL, cw), lambda i, c: (0, c)),
                  pl.BlockSpec((D_MODEL, cw), lambda i, c: (0, c)),
                  pl.BlockSpec((cw, D_MODEL), lambda i, c: (c, 0)),
                  const((D_MODEL, D_SHARED)), const((D_MODEL, D_SHARED)), const((D_SHARED, D_MODEL)),
                  const((1, D_MODEL)), const((1, D_MODEL))],
        out_specs=pl.BlockSpec((tm, D_MODEL), lambda i, c: (i, 0)),
        out_shape=jax.ShapeDtypeStruct((n, D_MODEL), F32),
        scratch_shapes=[pltpu.VMEM((tm, D_MODEL), BF16),
                        pltpu.VMEM((nchunk, MOE_CHUNK_EXPERTS, tm), F32),
                        pltpu.VMEM((tm, D_MODEL), F32)],
        compiler_params=_cparams(("parallel", "arbitrary")),
        name="moe",
    )(x, wr_t, br, wg, wu, wd, sg, su, sd, g, b)


def _prep_weights(w_in, gla_w_a2, gla_b_a2, gla_norm_g, sg_ln_g, sg_ln_b, sg_w, sg_b, w_branch_a, w_branch_b,
                  w_branch_c, w_out, ln1_g, ln1_b, ln2_g, ln2_b, moe_w_router, moe_b_router, moe_w_gate,
                  moe_w_up, moe_w_down, sh_w_gate, sh_w_up, sh_w_down, dec_seq):
    depth = w_in.shape[0]
    c_alr = 2 * GLA_QK + 2 * GLA_V
    w_perm = jnp.concatenate(
        [w_in[:, :, :c_alr], w_in[:, :, c_alr + GLA_RANK:], w_in[:, :, c_alr:c_alr + GLA_RANK],
         jnp.zeros((depth, D_MODEL, LANES - GLA_RANK), w_in.dtype)], axis=2).astype(BF16)
    w2 = jnp.concatenate([gla_w_a2, jnp.zeros((depth, LANES - GLA_RANK, GLA_QK), F32)], axis=1).astype(BF16)
    reps = PAGE_SIZE // dec_seq if dec_seq < SG_CHUNK else 1
    nrow_s = 256
    tile_s = nrow_s // dec_seq
    return dict(
        w_perm=w_perm,
        w2=w2, w2t=jnp.swapaxes(w2, 1, 2),
        brow=gla_b_a2[:, None, :], bcol=gla_b_a2[:, :, None],
        gla_g=gla_norm_g[:, None, :],
        sg_g=sg_ln_g[:, None, :], sg_b=sg_ln_b[:, None, :],
        sg_w_p=jnp.tile(sg_w, (1, 1, SG_CHUNKS_PER_STEP, SG_CHUNKS_PER_STEP)),
        sg_bias_p=jnp.tile(jnp.repeat(jnp.swapaxes(sg_b, 1, 2), SG_GC, axis=2), (1, SG_CHUNKS_PER_STEP, 1)),
        sg_w_s=jnp.tile(sg_w[:, :, :dec_seq, :dec_seq], (1, 1, tile_s, tile_s)),
        sg_bias_s=jnp.tile(jnp.repeat(jnp.swapaxes(sg_b[:, :, :dec_seq], 1, 2), SG_GC, axis=2), (1, tile_s, 1)),
        pa=w_branch_a.astype(BF16), pb=w_branch_b.astype(BF16), pc=w_branch_c.astype(BF16),
        wo=w_out.astype(BF16),
        ln1_g=ln1_g[:, None, :], ln1_b=ln1_b[:, None, :], ln2_g=ln2_g[:, None, :], ln2_b=ln2_b[:, None, :],
        wr_t=jnp.swapaxes(moe_w_router, 1, 2), br=moe_b_router[:, :, None],
        wg=jnp.transpose(moe_w_gate, (0, 2, 1, 3)).reshape(depth, D_MODEL, N_EXPERTS * D_EXPERT).astype(BF16),
        wu=jnp.transpose(moe_w_up, (0, 2, 1, 3)).reshape(depth, D_MODEL, N_EXPERTS * D_EXPERT).astype(BF16),
        wd=moe_w_down.reshape(depth, N_EXPERTS * D_EXPERT, D_MODEL).astype(BF16),
        sg=sh_w_gate.astype(BF16), su=sh_w_up.astype(BF16), sd=sh_w_down.astype(BF16),
    )


def _token_tail(x, h, oa, ob, oc, w, l):
    x1 = _merge(oa, ob, oc, h, x, w["pa"][l], w["pb"][l], w["pc"][l], w["wo"][l], w["ln1_g"][l], w["ln1_b"][l])
    return _moe(x1, w["wr_t"][l], w["br"][l], w["wg"][l], w["wu"][l], w["wd"][l], w["sg"][l], w["su"][l],
                w["sd"][l], w["ln2_g"][l], w["ln2_b"][l])


def kernel(x_prompt, x_sample, cache_k, cache_v, state_gla, page_table, w_in, gla_w_a2, gla_b_a2, gla_norm_g,
           sg_ln_g, sg_ln_b, sg_w, sg_b, w_branch_a, w_branch_b, w_branch_c, w_out, ln1_g, ln1_b, ln2_g, ln2_b,
           moe_w_router, moe_b_router, moe_w_gate, moe_w_up, moe_w_down, sh_w_gate, sh_w_up, sh_w_down):
    bp, sp, d = x_prompt.shape
    db, t, _ = x_sample.shape
    depth = w_in.shape[0]
    assert d == D_MODEL and sp % MOBA_BLOCK == 0 and sp % SG_CHUNK == 0 and t <= SUBLANES
    assert (db * t) % 256 == 0 or db * t == 256
    w = _prep_weights(w_in, gla_w_a2, gla_b_a2, gla_norm_g, sg_ln_g, sg_ln_b, sg_w, sg_b, w_branch_a,
                      w_branch_b, w_branch_c, w_out, ln1_g, ln1_b, ln2_g, ln2_b, moe_w_router, moe_b_router,
                      moe_w_gate, moe_w_up, moe_w_down, sh_w_gate, sh_w_up, sh_w_down, t)
    cache_kt = jnp.swapaxes(cache_k, 3, 4)
    cache_vt = jnp.swapaxes(cache_v, 3, 4)
    xp = x_prompt.reshape(bp * sp, d)
    xs = x_sample.reshape(db * t, d)
    zero_state = jnp.zeros((bp, GLA_HEADS, GLA_DK, GLA_DV), F32)
    gla_p, gla_s, kp_l, vp_l, ks_l, vs_l, sgu_l = [], [], [], [], [], [], []
    for l in range(depth):
        gla_args = (w["w2"][l], w["w2t"][l], w["brow"][l], w["bcol"][l], w["gla_g"][l])
        hp = _inproj(xp, w["w_perm"][l])
        oa, s_p = _gla(hp, zero_state, *gla_args, bp, sp, math.gcd(bp, GLA_SEQS_PER_STEP))
        ob, kt, vt = _moba_prompt(hp, bp, sp)
        oc, _ = _sgu(hp, w["sg_g"][l], w["sg_b"][l], w["sg_w_p"][l], w["sg_bias_p"][l],
                     SG_CHUNKS_PER_STEP * SG_CHUNK, SG_CHUNK, False)
        xp = _token_tail(xp, hp, oa, ob, oc, w, l)
        hs = _inproj(xs, w["w_perm"][l])
        oa, s_s = _gla(hs, state_gla[l], *gla_args, db, t, math.gcd(db, GLA_SEQS_PER_STEP))
        ob = _moba_sample(hs, page_table, cache_kt, cache_vt, l, db, t)
        oc, vcs = _sgu(hs, w["sg_g"][l], w["sg_b"][l], w["sg_w_s"][l], w["sg_bias_s"][l], db * t, t, True)
        xs = _token_tail(xs, hs, oa, ob, oc, w, l)
        gla_p.append(s_p)
        gla_s.append(s_s)
        kp_l.append(jnp.swapaxes(kt, 3, 4))
        vp_l.append(jnp.swapaxes(vt, 3, 4))
        kb_s = hs[:, COL_KB * LANES:COL_KB * LANES + MOBA_W].reshape(db, t, MOBA_HEADS, MOBA_DH)
        vb_s = hs[:, COL_VB * LANES:COL_VB * LANES + MOBA_W].reshape(db, t, MOBA_HEADS, MOBA_DH)
        ks_l.append(kb_s.transpose(0, 2, 1, 3))
        vs_l.append(vb_s.transpose(0, 2, 1, 3))
        sgu_l.append(vcs.reshape(db, t, SG_W))
    return (xp.reshape(bp, sp, d), xs.reshape(db, t, d), jnp.stack(gla_p), jnp.stack(gla_s),
            jnp.stack(kp_l), jnp.stack(vp_l), jnp.stack(ks_l), jnp.stack(vs_l), jnp.stack(sgu_l))
```

```python
import functools
import math

import numpy as np
import jax
import jax.numpy as jnp
from jax import lax
from jax.experimental import pallas as pl
from jax.experimental.pallas import tpu as pltpu

F32 = jnp.float32
BF16 = jnp.bfloat16

D_MODEL = 1024
DEPTH = 4
PAGE_SIZE = 128
GLA_HEADS, GLA_DK, GLA_DV, GLA_RANK, GLA_TAU = 4, 64, 128, 16, 16.0
GLA_QK, GLA_V = GLA_HEADS * GLA_DK, GLA_HEADS * GLA_DV
MOBA_HEADS, MOBA_DH, MOBA_BLOCK, MOBA_TOPK = 8, 64, 256, 3
MOBA_W = MOBA_HEADS * MOBA_DH
SG_GROUPS, SG_GC, SG_CHUNK = 4, 128, 128
SG_W = SG_GROUPS * SG_GC
N_EXPERTS, TOP_K, N_EXPERT_GROUPS, TOPK_GROUPS = 64, 8, 8, 4
D_EXPERT, D_SHARED, ROUTED_SCALE = 128, 128, 2.5
ALPHA = (2 * DEPTH) ** 0.25
LN_EPS = 1e-5

LANES = 128
SUBLANES = 8
VMEM_LIMIT_BYTES = 56 * 1024 * 1024

N_IN_PERM = 2 * GLA_QK + 2 * GLA_V + 3 * MOBA_W + 2 * SG_W + 3 * D_MODEL + LANES
COL_QB, COL_KB, COL_VB = 12, 16, 20
COL_ALR = 56
GLA_SUB = 16
GLA_SEQS_PER_STEP = 4
MOBA_PAIRS_PER_STEP = 4
SG_CHUNKS_PER_STEP = 4
NEG_INF = float("-inf")


def _row_tile(n, cap):
    t = cap
    while n % t:
        t //= 2
    return t


def _cparams(sem):
    return pltpu.CompilerParams(dimension_semantics=sem, vmem_limit_bytes=VMEM_LIMIT_BYTES)


def _layer_norm(x, g, b):
    mu = jnp.mean(x, -1, keepdims=True)
    xc = x - mu
    var = jnp.mean(xc * xc, -1, keepdims=True)
    return xc * lax.rsqrt(var + LN_EPS) * g + b


def _log_sigmoid(z):
    return jnp.minimum(z, 0.0) - jnp.log1p(jnp.exp(-jnp.abs(z)))


def _silu(x):
    return x * jax.nn.sigmoid(x)


def _gelu_exact(x):
    return 0.5 * x * (1.0 + lax.erf(x * np.float32(np.sqrt(0.5))))


def _dot(a, b, precision=None):
    return jnp.dot(a, b, preferred_element_type=F32, precision=precision)


def _dot_nt(a, b, precision=None):
    return lax.dot_general(a, b, (((1,), (1,)), ((), ())), preferred_element_type=F32, precision=precision)


def _dot_tn(a, b, precision=None):
    return lax.dot_general(a, b, (((0,), (0,)), ((), ())), preferred_element_type=F32, precision=precision)


HIGHEST = lax.Precision.HIGHEST


def _inproj_kernel(x_ref, w_ref, o_ref, xb_ref):
    @pl.when(pl.program_id(1) == 0)
    def _():
        xb_ref[...] = x_ref[...].astype(BF16)

    o_ref[...] = _dot(xb_ref[...], w_ref[...])


def _layer_spec(shape, l, index_map=None):
    if index_map is None:
        return pl.BlockSpec((None,) + tuple(shape), lambda *idx: (l,) + (0,) * len(shape))
    return pl.BlockSpec((None,) + tuple(shape), lambda *idx: (l,) + tuple(index_map(*idx)))


def _inproj(x, w, l):
    n, d = x.shape
    nc = w.shape[2]
    tm = _row_tile(n, 1024)
    tn = nc // 3
    return pl.pallas_call(
        _inproj_kernel,
        grid=(n // tm, nc // tn),
        in_specs=[pl.BlockSpec((tm, d), lambda i, j: (i, 0)),
                  _layer_spec((d, tn), l, lambda i, j: (0, j))],
        out_specs=pl.BlockSpec((tm, tn), lambda i, j: (i, j)),
        out_shape=jax.ShapeDtypeStruct((n, nc), F32),
        scratch_shapes=[pltpu.VMEM((tm, d), BF16)],
        compiler_params=_cparams(("parallel", "arbitrary")),
        name="inproj",
    )(x, w)


def _gla_kernel(q_ref, k_ref, v_ref, r_ref, alr_ref, w2_ref, w2t_ref, brow_ref, bcol_ref, g_ref, s0_ref,
                o_ref, sfin_ref, s_ref, oraw_ref, *, nb, rows, pad):
    c = pl.program_id(1)
    nsteps = pl.num_programs(1)
    rr = rows + pad
    nblk = rr // GLA_SUB

    @pl.when(c == 0)
    def _():
        s_ref[...] = s0_ref[...]

    def padrows(a):
        if pad == 0:
            return a
        return jnp.concatenate([a, jnp.zeros((pad, a.shape[1]), a.dtype)], axis=0)

    ti = lax.broadcasted_iota(jnp.int32, (rr, rr), 0)
    si = lax.broadcasted_iota(jnp.int32, (rr, rr), 1)
    same = (ti // GLA_SUB) == (si // GLA_SUB)
    m_incl = jnp.where(same & (si <= ti), 1.0, 0.0).astype(F32)
    m_blk = jnp.where(same, 1.0, 0.0).astype(F32)
    bi_r = lax.broadcasted_iota(jnp.int32, (rr, LANES), 0)
    bi_c = lax.broadcasted_iota(jnp.int32, (rr, LANES), 1)
    m_ind = jnp.where((bi_r // GLA_SUB) == bi_c, 1.0, 0.0).astype(F32)
    he_r = lax.broadcasted_iota(jnp.int32, (GLA_QK, GLA_V), 0)
    he_c = lax.broadcasted_iota(jnp.int32, (GLA_QK, GLA_V), 1)
    he = jnp.where((he_r // GLA_DK) == (he_c // GLA_DV), 1.0, 0.0).astype(BF16)
    trow = lax.broadcasted_iota(jnp.int32, (GLA_SUB, 1), 0)

    seqs = []
    for b in range(nb):
        q = padrows(q_ref[b]) * np.float32(GLA_DK ** -0.5)
        k = padrows(k_ref[b])
        v = padrows(v_ref[b])
        alr = padrows(alr_ref[b]).astype(BF16)
        la = _log_sigmoid(_dot(alr, w2_ref[...]) + brow_ref[...]) * np.float32(1.0 / GLA_TAU)
        lat = _log_sigmoid(_dot_nt(w2t_ref[...], alr) + bcol_ref[...]) * np.float32(1.0 / GLA_TAU)
        if pad:
            la = jnp.where(lax.broadcasted_iota(jnp.int32, la.shape, 0) < rows, la, 0.0)
            lat = jnp.where(lax.broadcasted_iota(jnp.int32, lat.shape, 1) < rows, lat, 0.0)
        b_loc = _dot(m_incl, la, HIGHEST)
        b_tot = _dot(m_blk, la, HIGHEST)
        seqs.append(dict(
            q=q, k=k, v=v, b_loc=b_loc,
            qd=(q * jnp.exp(b_loc)).astype(BF16),
            kd=(k * jnp.exp(b_tot - b_loc)).astype(BF16),
            vb=v.astype(BF16),
            dcol=jnp.exp(_dot(lat, m_ind, HIGHEST)),
            s=[s_ref[b, h] for h in range(GLA_HEADS)]))

    for i in range(nblk):
        r0 = i * GLA_SUB
        for b in range(nb):
            sq = seqs[b]
            b_i = sq["b_loc"][r0:r0 + GLA_SUB]
            q_i = sq["q"][r0:r0 + GLA_SUB]
            k_i = sq["k"][r0:r0 + GLA_SUB]
            v_i = sq["v"][r0:r0 + GLA_SUB]
            half = GLA_SUB // 2
            parts = []
            for s in range(GLA_SUB):
                t0 = 0 if s < half else half
                e = jnp.exp(jnp.minimum(b_i[t0:] - b_i[s:s + 1], 0.0))
                parts.append(jnp.where(trow[t0:] >= s, q_i[t0:] * e * k_i[s:s + 1], 0.0))
            pstack = jnp.concatenate(parts, axis=0).astype(BF16)
            rexp = _dot(pstack, he)
            o_blk = rexp[0:GLA_SUB] * v_i[0:1]
            for s in range(1, half):
                o_blk = o_blk + rexp[s * GLA_SUB:(s + 1) * GLA_SUB] * v_i[s:s + 1]
            base = half * GLA_SUB
            o_hi = rexp[base:base + half] * v_i[half:half + 1]
            for s in range(half + 1, GLA_SUB):
                r1 = base + (s - half) * half
                o_hi = o_hi + rexp[r1:r1 + half] * v_i[s:s + 1]
            o_blk = o_blk + jnp.concatenate([jnp.zeros((half, GLA_V), F32), o_hi], axis=0)
            outs = []
            for h in range(GLA_HEADS):
                ks = slice(h * GLA_DK, (h + 1) * GLA_DK)
                vs = slice(h * GLA_DV, (h + 1) * GLA_DV)
                s_h = sq["s"][h]
                outs.append(_dot(sq["qd"][r0:r0 + GLA_SUB, ks], s_h.astype(BF16)))
                upd = _dot_tn(sq["kd"][r0:r0 + GLA_SUB, ks], sq["vb"][r0:r0 + GLA_SUB, vs])
                sq["s"][h] = sq["dcol"][ks, i:i + 1] * s_h + upd
            oraw_ref[b, r0:r0 + GLA_SUB, :] = o_blk + jnp.concatenate(outs, axis=1)

    g = g_ref[...]
    for b in range(nb):
        for h in range(GLA_HEADS):
            s_ref[b, h] = seqs[b]["s"][h]
        o = oraw_ref[b, 0:rows, :]
        normed = []
        for h in range(GLA_HEADS):
            oh = o[:, h * GLA_DV:(h + 1) * GLA_DV]
            ms = jnp.mean(oh * oh, -1, keepdims=True)
            normed.append(oh * lax.rsqrt(ms + 1e-6) * g)
        o_ref[b] = jnp.concatenate(normed, axis=1) * _silu(r_ref[b])

    @pl.when(c == nsteps - 1)
    def _():
        sfin_ref[...] = s_ref[...]


def _gla(h, s0, w2, w2t, brow, bcol, g, nseq, seqlen, nb):
    rows = min(seqlen, 128)
    pad = (-rows) % GLA_SUB
    steps = seqlen // rows
    h3 = h.reshape(nseq, seqlen, h.shape[1])
    blk = lambda w, j: pl.BlockSpec((nb, rows, w), lambda b, c: (b, c, j))
    const = lambda shp: pl.BlockSpec(shp, lambda b, c: (0,) * len(shp))
    state = pl.BlockSpec((nb, GLA_HEADS, GLA_DK, GLA_DV), lambda b, c: (b, 0, 0, 0))
    o, sfin = pl.pallas_call(
        functools.partial(_gla_kernel, nb=nb, rows=rows, pad=pad),
        grid=(nseq // nb, steps),
        in_specs=[blk(GLA_QK, 0), blk(GLA_QK, 1), blk(GLA_V, 1), blk(GLA_V, 2), blk(LANES, COL_ALR),
                  const((LANES, GLA_QK)), const((GLA_QK, LANES)), const((1, GLA_QK)), const((GLA_QK, 1)),
                  const((1, GLA_DV)), state],
        out_specs=[pl.BlockSpec((nb, rows, GLA_V), lambda b, c: (b, c, 0)), state],
        out_shape=[jax.ShapeDtypeStruct((nseq, seqlen, GLA_V), F32),
                   jax.ShapeDtypeStruct((nseq, GLA_HEADS, GLA_DK, GLA_DV), F32)],
        scratch_shapes=[pltpu.VMEM((nb, GLA_HEADS, GLA_DK, GLA_DV), F32),
                        pltpu.VMEM((nb, rows + pad, GLA_V), F32)],
        compiler_params=_cparams(("parallel", "arbitrary")),
        name="gla",
    )(h3, h3, h3, h3, h3, w2, w2t, brow, bcol, g, s0)
    return o.reshape(nseq * seqlen, GLA_V), sfin


def _moba_prompt_kernel(q_ref, k_ref, v_ref, o_ref, kt_ref, vt_ref, kb_ref, vtb_ref, km_ref, sel_ref, s_ref,
                        acc_ref, *, nblk, npg):
    i = pl.program_id(2)
    blk = MOBA_BLOCK
    nh = 2 * npg

    @pl.when(i == 0)
    def _():
        km_ref[...] = jnp.zeros(km_ref.shape, F32)
        kb_ref[...] = jnp.zeros(kb_ref.shape, BF16)
        vtb_ref[...] = jnp.zeros(vtb_ref.shape, BF16)

    kf = k_ref[...]
    kt = kf.T
    vt = v_ref[...].T
    k_own = kf.astype(BF16)
    vt_own = vt.astype(BF16)
    kb_ref[i] = k_own
    vtb_ref[i] = vt_own
    for pg in range(blk // PAGE_SIZE):
        kt_ref[0, pg] = kt[:, pg * PAGE_SIZE:(pg + 1) * PAGE_SIZE].reshape(nh, MOBA_DH, PAGE_SIZE)
        vt_ref[0, pg] = vt[:, pg * PAGE_SIZE:(pg + 1) * PAGE_SIZE].reshape(nh, MOBA_DH, PAGE_SIZE)

    qt = (q_ref[...] * np.float32(MOBA_DH ** -0.5)).T
    drow = lax.broadcasted_iota(jnp.int32, (LANES, blk), 0)
    brow = lax.broadcasted_iota(jnp.int32, (nblk, blk), 0)
    krow = lax.broadcasted_iota(jnp.int32, (blk, blk), 0)
    qcol = lax.broadcasted_iota(jnp.int32, (blk, blk), 1)
    km = km_ref[...]
    ones = jnp.ones((2 * SUBLANES, blk), BF16)

    def slab(a, hh):
        return a[:, (hh // 2) * LANES:(hh // 2 + 1) * LANES]

    def lhs_rows(vtj, hh):
        return jnp.concatenate([vtj[hh * MOBA_DH:(hh + 1) * MOBA_DH], ones], axis=0)

    qbs, ms = [], []
    for hh in range(nh):
        qm = jnp.where((drow // MOBA_DH) == (hh % 2), qt[(hh // 2) * LANES:(hh // 2 + 1) * LANES], 0.0)
        gate = _dot(slab(km, hh), qm, HIGHEST)
        gsel = jnp.where(brow < i, gate, NEG_INF)
        sel = jnp.zeros((nblk, blk), F32)
        for _ in range(MOBA_TOPK):
            m = jnp.max(gsel, axis=0, keepdims=True)
            idx = jnp.min(jnp.where(gsel == m, brow, nblk), axis=0, keepdims=True)
            hit = brow == idx
            sel = jnp.where(hit & (m > NEG_INF), 1.0, sel)
            gsel = jnp.where(hit, NEG_INF, gsel)
        sel_ref[hh] = sel
        qb = qm.astype(BF16)
        s = jnp.where(krow <= qcol, _dot(slab(k_own, hh), qb), NEG_INF)
        m0 = jnp.max(s, axis=0, keepdims=True)
        p = jnp.exp((s - m0).astype(BF16))
        acc_ref[hh] = _dot(lhs_rows(vt_own, hh), p)
        ms.append(m0)
        qbs.append(qb)

    km_ref[pl.ds(i, 1), :] = jnp.sum(kf, axis=0, keepdims=True) * np.float32(1.0 / blk)

    npairs = (i + 1) // 2

    def stage_a(t, slot):
        mx = [jnp.full((1, blk), NEG_INF, F32) for _ in range(nh)]
        for bb in range(2):
            j = jnp.minimum(2 * t + bb, nblk - 1)
            kj = kb_ref[j]
            for hh in range(nh):
                s = jnp.where(sel_ref[hh, pl.ds(j, 1), :] > 0.0, _dot(slab(kj, hh), qbs[hh]), NEG_INF)
                s_ref[hh, slot, bb] = s
                mx[hh] = jnp.maximum(mx[hh], jnp.max(s, axis=0, keepdims=True))
        return mx

    def stage_b(t, slot, ms, mx):
        ms = list(ms)
        vts = [vtb_ref[jnp.minimum(2 * t + bb, nblk - 1)] for bb in range(2)]
        for hh in range(nh):
            m_new = jnp.maximum(ms[hh], mx[hh])
            contrib = None
            for bb in range(2):
                p = jnp.exp((s_ref[hh, slot, bb] - m_new).astype(BF16))
                d = _dot(lhs_rows(vts[bb], hh), p)
                contrib = d if contrib is None else contrib + d
            acc_ref[hh] = jnp.exp(ms[hh] - m_new) * acc_ref[hh] + contrib
            ms[hh] = m_new
        return ms

    def body(u, c):
        ms, mx0 = c[:nh], c[nh:]
        mx1 = stage_a(2 * u + 1, 1)
        ms = stage_b(2 * u, 0, ms, mx0)
        mx0 = stage_a(2 * u + 2, 0)
        ms = stage_b(2 * u + 1, 1, ms, mx1)
        return (*ms, *mx0)

    nloop = jnp.maximum((npairs - 1) // 2, 0)
    c = lax.fori_loop(0, nloop, body, (*ms, *stage_a(0, 0)))
    mx1 = stage_a(2 * nloop + 1, 1)
    ms = stage_b(2 * nloop, 0, c[:nh], c[nh:])
    stage_b(2 * nloop + 1, 1, ms, mx1)
    outs = []
    for hh in range(nh):
        a = acc_ref[hh]
        outs.append(a[:MOBA_DH] / a[MOBA_DH:MOBA_DH + 1])
    o_ref[...] = jnp.concatenate(outs, axis=0).T


def _moba_prompt(h, nseq, seqlen):
    nblk = seqlen // MOBA_BLOCK
    npg = MOBA_PAIRS_PER_STEP
    ngrp = MOBA_HEADS // 2 // npg
    nh = 2 * npg
    w = npg * LANES
    ppb = MOBA_BLOCK // PAGE_SIZE
    cq, ck, cv = COL_QB // npg, COL_KB // npg, COL_VB // npg
    o, kt, vt = pl.pallas_call(
        functools.partial(_moba_prompt_kernel, nblk=nblk, npg=npg),
        grid=(nseq, ngrp, nblk),
        in_specs=[pl.BlockSpec((MOBA_BLOCK, w), lambda b, p, i: (b * nblk + i, cq + p)),
                  pl.BlockSpec((MOBA_BLOCK, w), lambda b, p, i: (b * nblk + i, ck + p)),
                  pl.BlockSpec((MOBA_BLOCK, w), lambda b, p, i: (b * nblk + i, cv + p))],
        out_specs=[pl.BlockSpec((MOBA_BLOCK, w), lambda b, p, i: (b * nblk + i, p)),
                   pl.BlockSpec((1, ppb, nh, MOBA_DH, PAGE_SIZE), lambda b, p, i: (b, i, p, 0, 0)),
                   pl.BlockSpec((1, ppb, nh, MOBA_DH, PAGE_SIZE), lambda b, p, i: (b, i, p, 0, 0))],
        out_shape=[jax.ShapeDtypeStruct((nseq * seqlen, MOBA_W), F32),
                   jax.ShapeDtypeStruct((nseq, seqlen // PAGE_SIZE, MOBA_HEADS, MOBA_DH, PAGE_SIZE), F32),
                   jax.ShapeDtypeStruct((nseq, seqlen // PAGE_SIZE, MOBA_HEADS, MOBA_DH, PAGE_SIZE), F32)],
        scratch_shapes=[pltpu.VMEM((nblk, MOBA_BLOCK, w), BF16),
                        pltpu.VMEM((nblk, w, MOBA_BLOCK), BF16),
                        pltpu.VMEM((nblk, w), F32),
                        pltpu.VMEM((nh, nblk, MOBA_BLOCK), F32),
                        pltpu.VMEM((nh, 2, 2, MOBA_BLOCK, MOBA_BLOCK), F32),
                        pltpu.VMEM((nh, MOBA_DH + 2 * SUBLANES, MOBA_BLOCK), F32)],
        compiler_params=_cparams(("parallel", "parallel", "arbitrary")),
        name="moba_prompt",
    )(h, h, h)
    return o, kt, vt


MOBA_RING = 3
MOBA_GROUP_MAX = 8


def _moba_group(npages):
    return min(MOBA_GROUP_MAX, npages // 2)


def _moba_sample_kernel(pt_ref, q_ref, kn_ref, vn_ref, kc_ref, vc_ref, o_ref,
                        buf_ref, sem_ref, sall_ref, *, layer, npages, tq):
    b = pl.program_id(0)
    hd = MOBA_HEADS * MOBA_DH
    nrow = MOBA_HEADS * tq
    ppb = MOBA_BLOCK // PAGE_SIZE
    nfull = npages // ppb
    grp = _moba_group(npages)
    ngk = npages // grp
    ahead = MOBA_RING - 1

    nseq = pl.num_programs(0)
    base = b * (2 * ngk)

    def ring(g_stream):
        return ((base + g_stream) % MOBA_RING) * grp

    def copy(src_ref, seq, page, slot):
        return pltpu.make_async_copy(src_ref.at[layer, pt_ref[seq, page]], buf_ref.at[slot], sem_ref.at[slot])

    def start_group(src_ref, seq, g_local, g_stream):
        for k in range(grp):
            copy(src_ref, seq, g_local * grp + k, ring(g_stream) + k).start()

    def wait_group(src_ref, g_local, g_stream):
        for k in range(grp):
            copy(src_ref, b, g_local * grp + k, ring(g_stream) + k).wait()

    @pl.when(b == 0)
    def _():
        for g in range(ahead):
            start_group(kc_ref, b, g, g)

    qs = q_ref[...] * np.float32(MOBA_DH ** -0.5)
    qt = jnp.concatenate([qs] * MOBA_HEADS, axis=0)
    r_i = lax.broadcasted_iota(jnp.int32, (nrow, hd), 0)
    c_i = lax.broadcasted_iota(jnp.int32, (nrow, hd), 1)
    diag = (r_i // tq) == (c_i // MOBA_DH)
    qbd = jnp.where(diag, qt, 0.0)
    qbd_b = qbd.astype(BF16)
    gl = lax.broadcasted_iota(jnp.int32, (nrow, LANES), 1)

    def k_group(g, carry, next_src):
        gsum, pmax = carry
        nxt = g + ahead
        start_group(next_src, b, nxt if next_src is kc_ref else nxt - ngk, nxt)
        wait_group(kc_ref, g, g)
        for k in range(grp):
            j = g * grp + k
            kp = buf_ref[ring(g) + k].reshape(hd, PAGE_SIZE)
            s = _dot(qbd_b, kp.astype(BF16))
            sall_ref[j] = s
            gsum = gsum + jnp.where(gl == j // ppb, jnp.sum(s, -1, keepdims=True), 0.0)
            pmax = jnp.where(gl == j, jnp.max(s, -1, keepdims=True), pmax)
        return gsum, pmax

    carry = (jnp.zeros((nrow, LANES), F32), jnp.full((nrow, LANES), NEG_INF, F32))
    carry = lax.fori_loop(0, ngk - ahead, lambda g, c: k_group(g, c, kc_ref), carry)
    for g in range(ngk - ahead, ngk):
        carry = k_group(g, carry, vc_ref)
    gsum, pmax = carry

    gate = gsum * np.float32(1.0 / MOBA_BLOCK)
    gsel = jnp.where(gl < nfull, gate, NEG_INF)
    sel = jnp.zeros((nrow, LANES), F32)
    for _ in range(min(MOBA_TOPK, nfull)):
        m = jnp.max(gsel, -1, keepdims=True)
        idx = jnp.min(jnp.where(gsel == m, gl, LANES), -1, keepdims=True)
        hit = gl == idx
        sel = jnp.where(hit & (m > NEG_INF), 1.0, sel)
        gsel = jnp.where(hit, NEG_INF, gsel)

    kn = jnp.concatenate([kn_ref[...], jnp.zeros((LANES - tq, hd), F32)], axis=0).astype(BF16)
    vn = jnp.concatenate([vn_ref[...], jnp.zeros((LANES - tq, hd), F32)], axis=0).astype(BF16)
    s_new = _dot_nt(qbd_b, kn)
    nr = lax.broadcasted_iota(jnp.int32, (nrow, LANES), 0)
    s_new = jnp.where(gl <= (nr % tq), s_new, NEG_INF)

    eb_r = lax.broadcasted_iota(jnp.int32, (LANES, LANES), 0)
    eb_c = lax.broadcasted_iota(jnp.int32, (LANES, LANES), 1)
    expand = jnp.where((eb_c // ppb == eb_r) & (eb_c < npages), 1.0, 0.0).astype(BF16)
    selp = _dot(sel.astype(BF16), expand)
    m_all = jnp.maximum(jnp.max(jnp.where(selp > 0.0, pmax, NEG_INF), -1, keepdims=True),
                        jnp.max(s_new, -1, keepdims=True))
    p_new = jnp.exp(s_new - m_all)
    l_new = jnp.sum(p_new, -1, keepdims=True)
    acc0 = _dot(p_new.astype(BF16), vn)

    def shift_body(j, carry):
        col = jnp.max(jnp.where(gl == j, selp, 0.0), -1, keepdims=True)
        sall_ref[j] = jnp.where(col > 0.0, sall_ref[j] - m_all, NEG_INF)
        return carry

    lax.fori_loop(0, npages, shift_body, 0, unroll=4)

    def v_group(g, carry, prefetch):
        lsum, acc = carry
        if prefetch:
            start_group(vc_ref, b, g + ahead, ngk + g + ahead)
        else:
            @pl.when(b + 1 < nseq)
            def _():
                start_group(kc_ref, b + 1, g + ahead - ngk, ngk + g + ahead)
        wait_group(vc_ref, g, ngk + g)
        for k in range(grp):
            j = g * grp + k
            vp = buf_ref[ring(ngk + g) + k].reshape(hd, PAGE_SIZE).astype(BF16)
            p = jnp.exp(sall_ref[j])
            lsum = lsum + p
            acc = acc + _dot_nt(p.astype(BF16), vp)
        return lsum, acc

    carry = (jnp.zeros((nrow, LANES), F32), acc0)
    carry = lax.fori_loop(0, ngk - ahead, lambda g, c: v_group(g, c, True), carry)
    for g in range(ngk - ahead, ngk):
        carry = v_group(g, carry, False)
    lsum, acc = carry
    l_all = l_new + jnp.sum(lsum, -1, keepdims=True)
    res = jnp.where(diag, acc / l_all, 0.0).reshape(MOBA_HEADS, tq, hd)
    o_ref[...] = jnp.sum(res, axis=0)


def _moba_sample(h, page_table, cache_kt, cache_vt, layer, nseq, tq):
    npages = page_table.shape[1]
    hd = MOBA_W
    grp = _moba_group(npages)
    nslot = MOBA_RING * grp
    assert npages <= LANES and npages % (MOBA_BLOCK // PAGE_SIZE) == 0
    assert npages % grp == 0 and npages // grp >= MOBA_RING - 1
    grid_spec = pltpu.PrefetchScalarGridSpec(
        num_scalar_prefetch=1,
        grid=(nseq,),
        in_specs=[pl.BlockSpec((tq, hd), lambda b, pt: (b, COL_QB // 4)),
                  pl.BlockSpec((tq, hd), lambda b, pt: (b, COL_KB // 4)),
                  pl.BlockSpec((tq, hd), lambda b, pt: (b, COL_VB // 4)),
                  pl.BlockSpec(memory_space=pl.ANY),
                  pl.BlockSpec(memory_space=pl.ANY)],
        out_specs=pl.BlockSpec((tq, hd), lambda b, pt: (b, 0)),
        scratch_shapes=[pltpu.VMEM((nslot, MOBA_HEADS, MOBA_DH, PAGE_SIZE), F32),
                        pltpu.SemaphoreType.DMA((nslot,)),
                        pltpu.VMEM((npages, MOBA_HEADS * tq, PAGE_SIZE), F32)],
    )
    return pl.pallas_call(
        functools.partial(_moba_sample_kernel, layer=layer, npages=npages, tq=tq),
        grid_spec=grid_spec,
        out_shape=jax.ShapeDtypeStruct((nseq * tq, hd), F32),
        compiler_params=_cparams(("arbitrary",)),
        name="moba_sample",
    )(page_table, h, h, h, cache_kt, cache_vt)


def _sgu_kernel(u_ref, v_ref, g_ref, b_ref, w_ref, bias_ref, o_ref, *maybe_v_out, period):
    rows = u_ref.shape[0]
    u = _gelu_exact(u_ref[...])
    vn = _layer_norm(_gelu_exact(v_ref[...]), g_ref[...], b_ref[...])
    if maybe_v_out:
        maybe_v_out[0][...] = vn
    t = lax.broadcasted_iota(jnp.int32, (rows, rows), 0)
    s = lax.broadcasted_iota(jnp.int32, (rows, rows), 1)
    keep = ((t // period) == (s // period)) & ((s % period) <= (t % period))
    vb = vn.astype(BF16)
    mix = []
    for g in range(SG_GROUPS):
        wg = jnp.where(keep, w_ref[g], 0.0).astype(BF16)
        mix.append(_dot(wg, vb[:, g * SG_GC:(g + 1) * SG_GC]))
    o_ref[...] = u * (jnp.concatenate(mix, axis=1) + bias_ref[...])


def _sgu(h, ln_g, ln_b, wmix, bias, l, rows, period, want_v):
    n = h.shape[0]
    const = lambda shp: _layer_spec(shp, l)
    out_specs = [pl.BlockSpec((rows, SG_W), lambda i: (i, 0))]
    out_shape = [jax.ShapeDtypeStruct((n, SG_W), F32)]
    if want_v:
        out_specs.append(pl.BlockSpec((rows, SG_W), lambda i: (i, 0)))
        out_shape.append(jax.ShapeDtypeStruct((n, SG_W), F32))
    res = pl.pallas_call(
        functools.partial(_sgu_kernel, period=period),
        grid=(n // rows,),
        in_specs=[pl.BlockSpec((rows, SG_W), lambda i: (i, 6)),
                  pl.BlockSpec((rows, SG_W), lambda i: (i, 7)),
                  const((1, SG_W)), const((1, SG_W)),
                  const((SG_GROUPS, rows, rows)), const((rows, SG_W))],
        out_specs=out_specs,
        out_shape=out_shape,
        compiler_params=_cparams(("parallel",)),
        name="sgu",
    )(h, h, ln_g, ln_b, wmix, bias)
    return res if want_v else (res[0], None)


def _merge_kernel(oa_ref, ob_ref, oc_ref, ga_ref, gb_ref, gc_ref, x_ref, pa_ref, pb_ref, pc_ref, wo_ref,
                  g_ref, b_ref, o_ref):
    m = (jax.nn.sigmoid(ga_ref[...]) * _dot(oa_ref[...].astype(BF16), pa_ref[...])
         + jax.nn.sigmoid(gb_ref[...]) * _dot(ob_ref[...].astype(BF16), pb_ref[...])
         + jax.nn.sigmoid(gc_ref[...]) * _dot(oc_ref[...].astype(BF16), pc_ref[...]))
    y = np.float32(ALPHA) * x_ref[...] + _dot(m.astype(BF16), wo_ref[...])
    o_ref[...] = _layer_norm(y, g_ref[...], b_ref[...])


def _merge(oa, ob, oc, h, x, pa, pb, pc, wo, g, b, l):
    n = x.shape[0]
    tm = _row_tile(n, 512)
    row = lambda w, j: pl.BlockSpec((tm, w), lambda i: (i, j))
    const = lambda shp: _layer_spec(shp, l)
    return pl.pallas_call(
        _merge_kernel,
        grid=(n // tm,),
        in_specs=[row(GLA_V, 0), row(MOBA_W, 0), row(SG_W, 0),
                  row(D_MODEL, 4), row(D_MODEL, 5), row(D_MODEL, 6), row(D_MODEL, 0),
                  const((GLA_V, D_MODEL)), const((MOBA_W, D_MODEL)), const((SG_W, D_MODEL)),
                  const((D_MODEL, D_MODEL)), const((1, D_MODEL)), const((1, D_MODEL))],
        out_specs=row(D_MODEL, 0),
        out_shape=jax.ShapeDtypeStruct((n, D_MODEL), F32),
        compiler_params=_cparams(("parallel",)),
        name="merge",
    )(oa, ob, oc, h, h, h, x, pa, pb, pc, wo, g, b)


MOE_CHUNK_EXPERTS = 8


def _moe_kernel(x_ref, wr_ref, br_ref, wg_ref, wu_ref, wd_ref, sg_ref, su_ref, sd_ref, g_ref, b_ref,
                o_ref, xb_ref, gates_ref, acc_ref):
    c = pl.program_id(1)
    nchunk = pl.num_programs(1)
    tm = x_ref.shape[0]
    ce = MOE_CHUNK_EXPERTS
    gsz = N_EXPERTS // N_EXPERT_GROUPS

    @pl.when(c == 0)
    def _():
        x = x_ref[...]
        xb = x.astype(BF16)
        xb_ref[...] = xb
        wr = wr_ref[...]
        w_hi = wr.astype(BF16)
        w_lo = (wr - w_hi.astype(F32)).astype(BF16)
        x_lo = (x - xb.astype(F32)).astype(BF16)
        both = _dot_nt(jnp.concatenate([w_hi, w_lo], axis=0), xb)
        s = jax.nn.sigmoid(both[:N_EXPERTS] + both[N_EXPERTS:] + _dot_nt(w_hi, x_lo))
        sel = s + br_ref[...]
        sel3 = sel.reshape(N_EXPERT_GROUPS, gsz, tm)
        j3 = lax.broadcasted_iota(jnp.int32, sel3.shape, 1)
        m1 = jnp.max(sel3, axis=1, keepdims=True)
        i1 = jnp.min(jnp.where(sel3 == m1, j3, gsz), axis=1, keepdims=True)
        m2 = jnp.max(jnp.where(j3 == i1, NEG_INF, sel3), axis=1, keepdims=True)
        grp = (m1 + m2).reshape(N_EXPERT_GROUPS, tm)
        gi = lax.broadcasted_iota(jnp.int32, grp.shape, 0)
        gkeep = jnp.zeros(grp.shape, F32)
        for _ in range(TOPK_GROUPS):
            m = jnp.max(grp, axis=0, keepdims=True)
            idx = jnp.min(jnp.where(grp == m, gi, N_EXPERT_GROUPS), axis=0, keepdims=True)
            hit = gi == idx
            gkeep = jnp.where(hit, 1.0, gkeep)
            grp = jnp.where(hit, NEG_INF, grp)
        emask = jnp.broadcast_to(gkeep.reshape(N_EXPERT_GROUPS, 1, tm), sel3.shape).reshape(N_EXPERTS, tm)
        cand = jnp.where(emask > 0.0, sel, NEG_INF)
        ei = lax.broadcasted_iota(jnp.int32, cand.shape, 0)
        chosen = jnp.zeros(cand.shape, F32)
        for _ in range(TOP_K):
            m = jnp.max(cand, axis=0, keepdims=True)
            idx = jnp.min(jnp.where(cand == m, ei, N_EXPERTS), axis=0, keepdims=True)
            hit = ei == idx
            chosen = jnp.where(hit, 1.0, chosen)
            cand = jnp.where(hit, NEG_INF, cand)
        w = jnp.where(chosen > 0.0, s, 0.0)
        gates_ref[...] = (w / jnp.sum(w, axis=0, keepdims=True) * np.float32(ROUTED_SCALE)).reshape(
            N_EXPERTS // ce, ce, tm)
        hs = _silu(_dot(xb, sg_ref[...])) * _dot(xb, su_ref[...])
        acc_ref[...] = _dot(hs.astype(BF16), sd_ref[...])

    xb = xb_ref[...]
    hh = _silu(_dot(xb, wg_ref[...])) * _dot(xb, wu_ref[...])
    gt = jnp.concatenate([gates_ref[c], jnp.zeros((LANES - ce, tm), F32)], axis=0).T
    gexp = jnp.concatenate([jnp.broadcast_to(gt[:, j:j + 1], (tm, D_EXPERT)) for j in range(ce)], axis=1)
    acc_ref[...] += _dot((hh * gexp).astype(BF16), wd_ref[...])

    @pl.when(c == nchunk - 1)
    def _():
        o_ref[...] = _layer_norm(np.float32(ALPHA) * x_ref[...] + acc_ref[...], g_ref[...], b_ref[...])


def _moe(x, wr_t, br, wg, wu, wd, sg, su, sd, g, b, l):
    n = x.shape[0]
    tm = _row_tile(n, 1024)
    cw = MOE_CHUNK_EXPERTS * D_EXPERT
    nchunk = N_EXPERTS // MOE_CHUNK_EXPERTS
    const = lambda shp: _layer_spec(shp, l)
    return pl.pallas_call(
        _moe_kernel,
        grid=(n // tm, nchunk),
        in_specs=[pl.BlockSpec((tm, D_MODEL), lambda i, c: (i, 0)),
                  const((N_EXPERTS, D_MODEL)), const((N_EXPERTS, 1)),
                  _layer_spec((D_MODEL, cw), l, lambda i, c: (0, c)),
                  _layer_spec((D_MODEL, cw), l, lambda i, c: (0, c)),
                  _layer_spec((cw, D_MODEL), l, lambda i, c: (c, 0)),
                  const((D_MODEL, D_SHARED)), const((D_MODEL, D_SHARED)), const((D_SHARED, D_MODEL)),
                  const((1, D_MODEL)), const((1, D_MODEL))],
        out_specs=pl.BlockSpec((tm, D_MODEL), lambda i, c: (i, 0)),
        out_shape=jax.ShapeDtypeStruct((n, D_MODEL), F32),
        scratch_shapes=[pltpu.VMEM((tm, D_MODEL), BF16),
                        pltpu.VMEM((nchunk, MOE_CHUNK_EXPERTS, tm), F32),
                        pltpu.VMEM((tm, D_MODEL), F32)],
        compiler_params=_cparams(("parallel", "arbitrary")),
        name="moe",
    )(x, wr_t, br, wg, wu, wd, sg, su, sd, g, b)


def _prep_weights(w_in, gla_w_a2, gla_b_a2, gla_norm_g, sg_ln_g, sg_ln_b, sg_w, sg_b, w_branch_a, w_branch_b,
                  w_branch_c, w_out, ln1_g, ln1_b, ln2_g, ln2_b, moe_w_router, moe_b_router, moe_w_gate,
                  moe_w_up, moe_w_down, sh_w_gate, sh_w_up, sh_w_down, dec_seq):
    depth = w_in.shape[0]
    c_alr = 2 * GLA_QK + 2 * GLA_V
    w_perm = jnp.concatenate(
        [w_in[:, :, :c_alr], w_in[:, :, c_alr + GLA_RANK:], w_in[:, :, c_alr:c_alr + GLA_RANK],
         jnp.zeros((depth, D_MODEL, LANES - GLA_RANK), w_in.dtype)], axis=2).astype(BF16)
    w2 = jnp.concatenate([gla_w_a2, jnp.zeros((depth, LANES - GLA_RANK, GLA_QK), F32)], axis=1).astype(BF16)
    reps = PAGE_SIZE // dec_seq if dec_seq < SG_CHUNK else 1
    nrow_s = 256
    tile_s = nrow_s // dec_seq
    return dict(
        w_perm=w_perm,
        w2=w2, w2t=jnp.swapaxes(w2, 1, 2),
        brow=gla_b_a2[:, None, :], bcol=gla_b_a2[:, :, None],
        gla_g=gla_norm_g[:, None, :],
        sg_g=sg_ln_g[:, None, :], sg_b=sg_ln_b[:, None, :],
        sg_w_p=jnp.tile(sg_w, (1, 1, SG_CHUNKS_PER_STEP, SG_CHUNKS_PER_STEP)),
        sg_bias_p=jnp.tile(jnp.repeat(jnp.swapaxes(sg_b, 1, 2), SG_GC, axis=2), (1, SG_CHUNKS_PER_STEP, 1)),
        sg_w_s=jnp.tile(sg_w[:, :, :dec_seq, :dec_seq], (1, 1, tile_s, tile_s)),
        sg_bias_s=jnp.tile(jnp.repeat(jnp.swapaxes(sg_b[:, :, :dec_seq], 1, 2), SG_GC, axis=2), (1, tile_s, 1)),
        pa=w_branch_a.astype(BF16), pb=w_branch_b.astype(BF16), pc=w_branch_c.astype(BF16),
        wo=w_out.astype(BF16),
        ln1_g=ln1_g[:, None, :], ln1_b=ln1_b[:, None, :], ln2_g=ln2_g[:, None, :], ln2_b=ln2_b[:, None, :],
        wr_t=jnp.swapaxes(moe_w_router, 1, 2), br=moe_b_router[:, :, None],
        wg=jnp.transpose(moe_w_gate, (0, 2, 1, 3)).reshape(depth, D_MODEL, N_EXPERTS * D_EXPERT).astype(BF16),
        wu=jnp.transpose(moe_w_up, (0, 2, 1, 3)).reshape(depth, D_MODEL, N_EXPERTS * D_EXPERT).astype(BF16),
        wd=moe_w_down.reshape(depth, N_EXPERTS * D_EXPERT, D_MODEL).astype(BF16),
        sg=sh_w_gate.astype(BF16), su=sh_w_up.astype(BF16), sd=sh_w_down.astype(BF16),
    )


def _token_tail(x, h, oa, ob, oc, w, l):
    x1 = _merge(oa, ob, oc, h, x, w["pa"], w["pb"], w["pc"], w["wo"], w["ln1_g"], w["ln1_b"], l)
    return _moe(x1, w["wr_t"], w["br"], w["wg"], w["wu"], w["wd"], w["sg"], w["su"], w["sd"], w["ln2_g"],
                w["ln2_b"], l)


def kernel(x_prompt, x_sample, cache_k, cache_v, state_gla, page_table, w_in, gla_w_a2, gla_b_a2, gla_norm_g,
           sg_ln_g, sg_ln_b, sg_w, sg_b, w_branch_a, w_branch_b, w_branch_c, w_out, ln1_g, ln1_b, ln2_g, ln2_b,
           moe_w_router, moe_b_router, moe_w_gate, moe_w_up, moe_w_down, sh_w_gate, sh_w_up, sh_w_down):
    bp, sp, d = x_prompt.shape
    db, t, _ = x_sample.shape
    depth = w_in.shape[0]
    assert d == D_MODEL and sp % MOBA_BLOCK == 0 and sp % SG_CHUNK == 0 and t <= SUBLANES
    assert (db * t) % 256 == 0 or db * t == 256
    w = _prep_weights(w_in, gla_w_a2, gla_b_a2, gla_norm_g, sg_ln_g, sg_ln_b, sg_w, sg_b, w_branch_a,
                      w_branch_b, w_branch_c, w_out, ln1_g, ln1_b, ln2_g, ln2_b, moe_w_router, moe_b_router,
                      moe_w_gate, moe_w_up, moe_w_down, sh_w_gate, sh_w_up, sh_w_down, t)
    cache_kt = jnp.swapaxes(cache_k, 3, 4)
    cache_vt = jnp.swapaxes(cache_v, 3, 4)
    xp = x_prompt.reshape(bp * sp, d)
    xs = x_sample.reshape(db * t, d)
    zero_state = jnp.zeros((bp, GLA_HEADS, GLA_DK, GLA_DV), F32)
    gla_p, gla_s, kp_l, vp_l, ks_l, vs_l, sgu_l = [], [], [], [], [], [], []
    for l in range(depth):
        gla_args = (w["w2"][l], w["w2t"][l], w["brow"][l], w["bcol"][l], w["gla_g"][l])
        hp = _inproj(xp, w["w_perm"], l)
        oa, s_p = _gla(hp, zero_state, *gla_args, bp, sp, math.gcd(bp, GLA_SEQS_PER_STEP))
        ob, kt, vt = _moba_prompt(hp, bp, sp)
        oc, _ = _sgu(hp, w["sg_g"], w["sg_b"], w["sg_w_p"], w["sg_bias_p"], l,
                     SG_CHUNKS_PER_STEP * SG_CHUNK, SG_CHUNK, False)
        xp = _token_tail(xp, hp, oa, ob, oc, w, l)
        hs = _inproj(xs, w["w_perm"], l)
        oa, s_s = _gla(hs, state_gla[l], *gla_args, db, t, math.gcd(db, GLA_SEQS_PER_STEP))
        ob = _moba_sample(hs, page_table, cache_kt, cache_vt, l, db, t)
        oc, vcs = _sgu(hs, w["sg_g"], w["sg_b"], w["sg_w_s"], w["sg_bias_s"], l, db * t, t, True)
        xs = _token_tail(xs, hs, oa, ob, oc, w, l)
        gla_p.append(s_p)
        gla_s.append(s_s)
        kp_l.append(jnp.swapaxes(kt, 3, 4))
        vp_l.append(jnp.swapaxes(vt, 3, 4))
        kb_s = hs[:, COL_KB * LANES:COL_KB * LANES + MOBA_W].reshape(db, t, MOBA_HEADS, MOBA_DH)
        vb_s = hs[:, COL_VB * LANES:COL_VB * LANES + MOBA_W].reshape(db, t, MOBA_HEADS, MOBA_DH)
        ks_l.append(kb_s.transpose(0, 2, 1, 3))
        vs_l.append(vb_s.transpose(0, 2, 1, 3))
        sgu_l.append(vcs.reshape(db, t, SG_W))
    return (xp.reshape(bp, sp, d), xs.reshape(db, t, d), jnp.stack(gla_p), jnp.stack(gla_s),
            jnp.stack(kp_l), jnp.stack(vp_l), jnp.stack(ks_l), jnp.stack(vs_l), jnp.stack(sgu_l))
```

```python
import functools
import math

import numpy as np
import jax
import jax.numpy as jnp
from jax import lax
from jax.experimental import pallas as pl
from jax.experimental.pallas import tpu as pltpu

F32 = jnp.float32
BF16 = jnp.bfloat16

D_MODEL = 1024
DEPTH = 4
PAGE_SIZE = 128
GLA_HEADS, GLA_DK, GLA_DV, GLA_RANK, GLA_TAU = 4, 64, 128, 16, 16.0
GLA_QK, GLA_V = GLA_HEADS * GLA_DK, GLA_HEADS * GLA_DV
MOBA_HEADS, MOBA_DH, MOBA_BLOCK, MOBA_TOPK = 8, 64, 256, 3
MOBA_W = MOBA_HEADS * MOBA_DH
SG_GROUPS, SG_GC, SG_CHUNK = 4, 128, 128
SG_W = SG_GROUPS * SG_GC
N_EXPERTS, TOP_K, N_EXPERT_GROUPS, TOPK_GROUPS = 64, 8, 8, 4
D_EXPERT, D_SHARED, ROUTED_SCALE = 128, 128, 2.5
ALPHA = (2 * DEPTH) ** 0.25
LN_EPS = 1e-5

LANES = 128
SUBLANES = 8
VMEM_LIMIT_BYTES = 56 * 1024 * 1024

N_IN_PERM = 2 * GLA_QK + 2 * GLA_V + 3 * MOBA_W + 2 * SG_W + 3 * D_MODEL + LANES
COL_QB, COL_KB, COL_VB = 12, 16, 20
COL_ALR = 56
GLA_SUB = 16
GLA_SEQS_PER_STEP = 4
MOBA_PAIRS_PER_STEP = 4
MOBA_SEQS_PER_STEP = 1
SG_CHUNKS_PER_STEP = 4
NEG_INF = float("-inf")


def _row_tile(n, cap):
    t = cap
    while n % t:
        t //= 2
    return t


def _cparams(sem):
    return pltpu.CompilerParams(dimension_semantics=sem, vmem_limit_bytes=VMEM_LIMIT_BYTES)


def _layer_norm(x, g, b):
    mu = jnp.mean(x, -1, keepdims=True)
    xc = x - mu
    var = jnp.mean(xc * xc, -1, keepdims=True)
    return xc * lax.rsqrt(var + LN_EPS) * g + b


def _log_sigmoid(z):
    return jnp.minimum(z, 0.0) - jnp.log1p(jnp.exp(-jnp.abs(z)))


def _silu(x):
    return x * jax.nn.sigmoid(x)


def _gelu_exact(x):
    return 0.5 * x * (1.0 + lax.erf(x * np.float32(np.sqrt(0.5))))


def _dot(a, b, precision=None):
    return jnp.dot(a, b, preferred_element_type=F32, precision=precision)


def _dot_nt(a, b, precision=None):
    return lax.dot_general(a, b, (((1,), (1,)), ((), ())), preferred_element_type=F32, precision=precision)


def _dot_tn(a, b, precision=None):
    return lax.dot_general(a, b, (((0,), (0,)), ((), ())), preferred_element_type=F32, precision=precision)


HIGHEST = lax.Precision.HIGHEST


def _inproj_kernel(x_ref, w_ref, o_ref, xb_ref):
    @pl.when(pl.program_id(1) == 0)
    def _():
        xb_ref[...] = x_ref[...].astype(BF16)

    o_ref[...] = _dot(xb_ref[...], w_ref[...])


def _layer_spec(shape, l, index_map=None):
    if index_map is None:
        return pl.BlockSpec((None,) + tuple(shape), lambda *idx: (l,) + (0,) * len(shape))
    return pl.BlockSpec((None,) + tuple(shape), lambda *idx: (l,) + tuple(index_map(*idx)))


def _inproj(x, w, l):
    n, d = x.shape
    nc = w.shape[2]
    tm = _row_tile(n, 1024)
    tn = nc // 3
    return pl.pallas_call(
        _inproj_kernel,
        grid=(n // tm, nc // tn),
        in_specs=[pl.BlockSpec((tm, d), lambda i, j: (i, 0)),
                  _layer_spec((d, tn), l, lambda i, j: (0, j))],
        out_specs=pl.BlockSpec((tm, tn), lambda i, j: (i, j)),
        out_shape=jax.ShapeDtypeStruct((n, nc), F32),
        scratch_shapes=[pltpu.VMEM((tm, d), BF16)],
        compiler_params=_cparams(("parallel", "arbitrary")),
        name="inproj",
    )(x, w)


def _gla_kernel(q_ref, k_ref, v_ref, r_ref, alr_ref, w2_ref, w2t_ref, brow_ref, bcol_ref, g_ref, s0_ref,
                o_ref, sfin_ref, s_ref, oraw_ref, *, nb, rows, pad):
    c = pl.program_id(1)
    nsteps = pl.num_programs(1)
    rr = rows + pad
    nblk = rr // GLA_SUB

    @pl.when(c == 0)
    def _():
        s_ref[...] = s0_ref[...]

    def padrows(a):
        if pad == 0:
            return a
        return jnp.concatenate([a, jnp.zeros((pad, a.shape[1]), a.dtype)], axis=0)

    ti = lax.broadcasted_iota(jnp.int32, (rr, rr), 0)
    si = lax.broadcasted_iota(jnp.int32, (rr, rr), 1)
    same = (ti // GLA_SUB) == (si // GLA_SUB)
    m_incl = jnp.where(same & (si <= ti), 1.0, 0.0).astype(F32)
    m_blk = jnp.where(same, 1.0, 0.0).astype(F32)
    bi_r = lax.broadcasted_iota(jnp.int32, (rr, LANES), 0)
    bi_c = lax.broadcasted_iota(jnp.int32, (rr, LANES), 1)
    m_ind = jnp.where((bi_r // GLA_SUB) == bi_c, 1.0, 0.0).astype(F32)
    he_r = lax.broadcasted_iota(jnp.int32, (GLA_QK, GLA_V), 0)
    he_c = lax.broadcasted_iota(jnp.int32, (GLA_QK, GLA_V), 1)
    he = jnp.where((he_r // GLA_DK) == (he_c // GLA_DV), 1.0, 0.0).astype(BF16)
    trow = lax.broadcasted_iota(jnp.int32, (GLA_SUB, 1), 0)

    seqs = []
    for b in range(nb):
        q = padrows(q_ref[b]) * np.float32(GLA_DK ** -0.5)
        k = padrows(k_ref[b])
        v = padrows(v_ref[b])
        alr = padrows(alr_ref[b]).astype(BF16)
        la = _log_sigmoid(_dot(alr, w2_ref[...]) + brow_ref[...]) * np.float32(1.0 / GLA_TAU)
        lat = _log_sigmoid(_dot_nt(w2t_ref[...], alr) + bcol_ref[...]) * np.float32(1.0 / GLA_TAU)
        if pad:
            la = jnp.where(lax.broadcasted_iota(jnp.int32, la.shape, 0) < rows, la, 0.0)
            lat = jnp.where(lax.broadcasted_iota(jnp.int32, lat.shape, 1) < rows, lat, 0.0)
        b_loc = _dot(m_incl, la, HIGHEST)
        b_tot = _dot(m_blk, la, HIGHEST)
        seqs.append(dict(
            q=q, k=k, v=v, b_loc=b_loc,
            qd=(q * jnp.exp(b_loc)).astype(BF16),
            kd=(k * jnp.exp(b_tot - b_loc)).astype(BF16),
            vb=v.astype(BF16),
            dcol=jnp.exp(_dot(lat, m_ind, HIGHEST)),
            s=[s_ref[b, h] for h in range(GLA_HEADS)]))

    for i in range(nblk):
        r0 = i * GLA_SUB
        for b in range(nb):
            sq = seqs[b]
            b_i = sq["b_loc"][r0:r0 + GLA_SUB]
            q_i = sq["q"][r0:r0 + GLA_SUB]
            k_i = sq["k"][r0:r0 + GLA_SUB]
            v_i = sq["v"][r0:r0 + GLA_SUB]
            half = GLA_SUB // 2
            parts = []
            for s in range(GLA_SUB):
                t0 = 0 if s < half else half
                e = jnp.exp(jnp.minimum(b_i[t0:] - b_i[s:s + 1], 0.0))
                parts.append(jnp.where(trow[t0:] >= s, q_i[t0:] * e * k_i[s:s + 1], 0.0))
            pstack = jnp.concatenate(parts, axis=0).astype(BF16)
            rexp = _dot(pstack, he)
            o_blk = rexp[0:GLA_SUB] * v_i[0:1]
            for s in range(1, half):
                o_blk = o_blk + rexp[s * GLA_SUB:(s + 1) * GLA_SUB] * v_i[s:s + 1]
            base = half * GLA_SUB
            o_hi = rexp[base:base + half] * v_i[half:half + 1]
            for s in range(half + 1, GLA_SUB):
                r1 = base + (s - half) * half
                o_hi = o_hi + rexp[r1:r1 + half] * v_i[s:s + 1]
            o_blk = o_blk + jnp.concatenate([jnp.zeros((half, GLA_V), F32), o_hi], axis=0)
            outs = []
            for h in range(GLA_HEADS):
                ks = slice(h * GLA_DK, (h + 1) * GLA_DK)
                vs = slice(h * GLA_DV, (h + 1) * GLA_DV)
                s_h = sq["s"][h]
                outs.append(_dot(sq["qd"][r0:r0 + GLA_SUB, ks], s_h.astype(BF16)))
                upd = _dot_tn(sq["kd"][r0:r0 + GLA_SUB, ks], sq["vb"][r0:r0 + GLA_SUB, vs])
                sq["s"][h] = sq["dcol"][ks, i:i + 1] * s_h + upd
            oraw_ref[b, r0:r0 + GLA_SUB, :] = o_blk + jnp.concatenate(outs, axis=1)

    g = g_ref[...]
    for b in range(nb):
        for h in range(GLA_HEADS):
            s_ref[b, h] = seqs[b]["s"][h]
        o = oraw_ref[b, 0:rows, :]
        normed = []
        for h in range(GLA_HEADS):
            oh = o[:, h * GLA_DV:(h + 1) * GLA_DV]
            ms = jnp.mean(oh * oh, -1, keepdims=True)
            normed.append(oh * lax.rsqrt(ms + 1e-6) * g)
        o_ref[b] = jnp.concatenate(normed, axis=1) * _silu(r_ref[b])

    @pl.when(c == nsteps - 1)
    def _():
        sfin_ref[...] = s_ref[...]


def _gla(h, s0, w2, w2t, brow, bcol, g, nseq, seqlen, nb):
    rows = min(seqlen, 128)
    pad = (-rows) % GLA_SUB
    steps = seqlen // rows
    h3 = h.reshape(nseq, seqlen, h.shape[1])
    blk = lambda w, j: pl.BlockSpec((nb, rows, w), lambda b, c: (b, c, j))
    const = lambda shp: pl.BlockSpec(shp, lambda b, c: (0,) * len(shp))
    state = pl.BlockSpec((nb, GLA_HEADS, GLA_DK, GLA_DV), lambda b, c: (b, 0, 0, 0))
    o, sfin = pl.pallas_call(
        functools.partial(_gla_kernel, nb=nb, rows=rows, pad=pad),
        grid=(nseq // nb, steps),
        in_specs=[blk(GLA_QK, 0), blk(GLA_QK, 1), blk(GLA_V, 1), blk(GLA_V, 2), blk(LANES, COL_ALR),
                  const((LANES, GLA_QK)), const((GLA_QK, LANES)), const((1, GLA_QK)), const((GLA_QK, 1)),
                  const((1, GLA_DV)), state],
        out_specs=[pl.BlockSpec((nb, rows, GLA_V), lambda b, c: (b, c, 0)), state],
        out_shape=[jax.ShapeDtypeStruct((nseq, seqlen, GLA_V), F32),
                   jax.ShapeDtypeStruct((nseq, GLA_HEADS, GLA_DK, GLA_DV), F32)],
        scratch_shapes=[pltpu.VMEM((nb, GLA_HEADS, GLA_DK, GLA_DV), F32),
                        pltpu.VMEM((nb, rows + pad, GLA_V), F32)],
        compiler_params=_cparams(("parallel", "arbitrary")),
        name="gla",
    )(h3, h3, h3, h3, h3, w2, w2t, brow, bcol, g, s0)
    return o.reshape(nseq * seqlen, GLA_V), sfin


def _moba_prompt_kernel(q_ref, k_ref, v_ref, o_ref, kt_ref, vt_ref, kb_ref, vtb_ref, km_ref, sel_ref, s_ref,
                        acc_ref, *, nblk, npg, nsq):
    i = pl.program_id(2)
    blk = MOBA_BLOCK
    nh = 2 * npg
    nch = nsq * nh

    @pl.when(i == 0)
    def _():
        km_ref[...] = jnp.zeros(km_ref.shape, F32)
        kb_ref[...] = jnp.zeros(kb_ref.shape, BF16)
        vtb_ref[...] = jnp.zeros(vtb_ref.shape, BF16)

    kfs, k_owns, vt_owns, qts, kms = [], [], [], [], []
    for sq in range(nsq):
        kf = k_ref[sq]
        kt = kf.T
        vt = v_ref[sq].T
        k_own = kf.astype(BF16)
        vt_own = vt.astype(BF16)
        kb_ref[sq, i] = k_own
        vtb_ref[sq, i] = vt_own
        for pg in range(blk // PAGE_SIZE):
            kt_ref[sq, pg] = kt[:, pg * PAGE_SIZE:(pg + 1) * PAGE_SIZE].reshape(nh, MOBA_DH, PAGE_SIZE)
            vt_ref[sq, pg] = vt[:, pg * PAGE_SIZE:(pg + 1) * PAGE_SIZE].reshape(nh, MOBA_DH, PAGE_SIZE)
        kfs.append(kf)
        k_owns.append(k_own)
        vt_owns.append(vt_own)
        qts.append((q_ref[sq] * np.float32(MOBA_DH ** -0.5)).T)
        kms.append(km_ref[sq])

    drow = lax.broadcasted_iota(jnp.int32, (LANES, blk), 0)
    brow = lax.broadcasted_iota(jnp.int32, (nblk, blk), 0)
    krow = lax.broadcasted_iota(jnp.int32, (blk, blk), 0)
    qcol = lax.broadcasted_iota(jnp.int32, (blk, blk), 1)
    ones = jnp.ones((2 * SUBLANES, blk), BF16)

    def slab(a, hh):
        return a[:, (hh // 2) * LANES:(hh // 2 + 1) * LANES]

    def lhs_rows(vtj, hh):
        return jnp.concatenate([vtj[hh * MOBA_DH:(hh + 1) * MOBA_DH], ones], axis=0)

    qbs, ms = [], []
    for g in range(nch):
        sq, hh = divmod(g, nh)
        qm = jnp.where((drow // MOBA_DH) == (hh % 2), qts[sq][(hh // 2) * LANES:(hh // 2 + 1) * LANES], 0.0)
        gate = _dot(slab(kms[sq], hh), qm, HIGHEST)
        gsel = jnp.where(brow < i, gate, NEG_INF)
        sel = jnp.zeros((nblk, blk), F32)
        for _ in range(MOBA_TOPK):
            m = jnp.max(gsel, axis=0, keepdims=True)
            idx = jnp.min(jnp.where(gsel == m, brow, nblk), axis=0, keepdims=True)
            hit = brow == idx
            sel = jnp.where(hit & (m > NEG_INF), 1.0, sel)
            gsel = jnp.where(hit, NEG_INF, gsel)
        sel_ref[g] = sel
        qb = qm.astype(BF16)
        s = jnp.where(krow <= qcol, _dot(slab(k_owns[sq], hh), qb), NEG_INF)
        m0 = jnp.max(s, axis=0, keepdims=True)
        p = jnp.exp((s - m0).astype(BF16))
        acc_ref[g] = _dot(lhs_rows(vt_owns[sq], hh), p)
        ms.append(m0)
        qbs.append(qb)

    for sq in range(nsq):
        km_ref[sq, pl.ds(i, 1), :] = jnp.sum(kfs[sq], axis=0, keepdims=True) * np.float32(1.0 / blk)

    npairs = (i + 1) // 2

    def stage_a(t, slot):
        mx = [jnp.full((1, blk), NEG_INF, F32) for _ in range(nch)]
        for bb in range(2):
            j = jnp.minimum(2 * t + bb, nblk - 1)
            for sq in range(nsq):
                kj = kb_ref[sq, j]
                for hh in range(nh):
                    g = sq * nh + hh
                    s = jnp.where(sel_ref[g, pl.ds(j, 1), :] > 0.0, _dot(slab(kj, hh), qbs[g]), NEG_INF)
                    s_ref[g, slot, bb] = s
                    mx[g] = jnp.maximum(mx[g], jnp.max(s, axis=0, keepdims=True))
        return mx

    def stage_b(t, slot, ms, mx):
        ms = list(ms)
        for sq in range(nsq):
            vts = [vtb_ref[sq, jnp.minimum(2 * t + bb, nblk - 1)] for bb in range(2)]
            for hh in range(nh):
                g = sq * nh + hh
                m_new = jnp.maximum(ms[g], mx[g])
                contrib = None
                for bb in range(2):
                    p = jnp.exp((s_ref[g, slot, bb] - m_new).astype(BF16))
                    d = _dot(lhs_rows(vts[bb], hh), p)
                    contrib = d if contrib is None else contrib + d
                acc_ref[g] = jnp.exp(ms[g] - m_new) * acc_ref[g] + contrib
                ms[g] = m_new
        return ms

    def body(u, c):
        ms, mx0 = c[:nch], c[nch:]
        mx1 = stage_a(2 * u + 1, 1)
        ms = stage_b(2 * u, 0, ms, mx0)
        mx0 = stage_a(2 * u + 2, 0)
        ms = stage_b(2 * u + 1, 1, ms, mx1)
        return (*ms, *mx0)

    nloop = jnp.maximum((npairs - 1) // 2, 0)
    c = lax.fori_loop(0, nloop, body, (*ms, *stage_a(0, 0)))
    mx1 = stage_a(2 * nloop + 1, 1)
    ms = stage_b(2 * nloop, 0, c[:nch], c[nch:])
    stage_b(2 * nloop + 1, 1, ms, mx1)
    for sq in range(nsq):
        outs = []
        for hh in range(nh):
            a = acc_ref[sq * nh + hh]
            outs.append(a[:MOBA_DH] / a[MOBA_DH:MOBA_DH + 1])
        o_ref[sq] = jnp.concatenate(outs, axis=0).T


def _moba_prompt(h, nseq, seqlen):
    nblk = seqlen // MOBA_BLOCK
    npg = MOBA_PAIRS_PER_STEP
    nsq = math.gcd(nseq, MOBA_SEQS_PER_STEP)
    ngrp = MOBA_HEADS // 2 // npg
    nh = 2 * npg
    nch = nsq * nh
    w = npg * LANES
    ppb = MOBA_BLOCK // PAGE_SIZE
    cq, ck, cv = COL_QB // npg, COL_KB // npg, COL_VB // npg
    h3 = h.reshape(nseq, seqlen, h.shape[1])
    o, kt, vt = pl.pallas_call(
        functools.partial(_moba_prompt_kernel, nblk=nblk, npg=npg, nsq=nsq),
        grid=(nseq // nsq, ngrp, nblk),
        in_specs=[pl.BlockSpec((nsq, MOBA_BLOCK, w), lambda b, p, i: (b, i, cq + p)),
                  pl.BlockSpec((nsq, MOBA_BLOCK, w), lambda b, p, i: (b, i, ck + p)),
                  pl.BlockSpec((nsq, MOBA_BLOCK, w), lambda b, p, i: (b, i, cv + p))],
        out_specs=[pl.BlockSpec((nsq, MOBA_BLOCK, w), lambda b, p, i: (b, i, p)),
                   pl.BlockSpec((nsq, ppb, nh, MOBA_DH, PAGE_SIZE), lambda b, p, i: (b, i, p, 0, 0)),
                   pl.BlockSpec((nsq, ppb, nh, MOBA_DH, PAGE_SIZE), lambda b, p, i: (b, i, p, 0, 0))],
        out_shape=[jax.ShapeDtypeStruct((nseq, seqlen, MOBA_W), F32),
                   jax.ShapeDtypeStruct((nseq, seqlen // PAGE_SIZE, MOBA_HEADS, MOBA_DH, PAGE_SIZE), F32),
                   jax.ShapeDtypeStruct((nseq, seqlen // PAGE_SIZE, MOBA_HEADS, MOBA_DH, PAGE_SIZE), F32)],
        scratch_shapes=[pltpu.VMEM((nsq, nblk, MOBA_BLOCK, w), BF16),
                        pltpu.VMEM((nsq, nblk, w, MOBA_BLOCK), BF16),
                        pltpu.VMEM((nsq, nblk, w), F32),
                        pltpu.VMEM((nch, nblk, MOBA_BLOCK), F32),
                        pltpu.VMEM((nch, 2, 2, MOBA_BLOCK, MOBA_BLOCK), F32),
                        pltpu.VMEM((nch, MOBA_DH + 2 * SUBLANES, MOBA_BLOCK), F32)],
        compiler_params=_cparams(("parallel", "parallel", "arbitrary")),
        name="moba_prompt",
    )(h3, h3, h3)
    return o.reshape(nseq * seqlen, MOBA_W), kt, vt


MOBA_RING = 5
MOBA_GROUP_MAX = 8


def _moba_group(npages):
    return min(MOBA_GROUP_MAX, npages // (MOBA_RING - 1))


def _moba_sample_kernel(pt_ref, q_ref, kn_ref, vn_ref, kc_ref, vc_ref, o_ref,
                        buf_ref, sem_ref, sall_ref, *, layer, npages, tq):
    b = pl.program_id(0)
    hd = MOBA_HEADS * MOBA_DH
    nrow = MOBA_HEADS * tq
    ppb = MOBA_BLOCK // PAGE_SIZE
    nfull = npages // ppb
    grp = _moba_group(npages)
    ngk = npages // grp
    ahead = MOBA_RING - 1

    nseq = pl.num_programs(0)
    base = b * (2 * ngk)

    def ring(g_stream):
        return ((base + g_stream) % MOBA_RING) * grp

    def copy(src_ref, seq, page, slot):
        return pltpu.make_async_copy(src_ref.at[layer, pt_ref[seq, page]], buf_ref.at[slot], sem_ref.at[slot])

    def start_group(src_ref, seq, g_local, g_stream):
        for k in range(grp):
            copy(src_ref, seq, g_local * grp + k, ring(g_stream) + k).start()

    def wait_group(src_ref, g_local, g_stream):
        for k in range(grp):
            copy(src_ref, b, g_local * grp + k, ring(g_stream) + k).wait()

    @pl.when(b == 0)
    def _():
        for g in range(ahead):
            start_group(kc_ref, b, g, g)

    qs = q_ref[...] * np.float32(MOBA_DH ** -0.5)
    qt = jnp.concatenate([qs] * MOBA_HEADS, axis=0)
    r_i = lax.broadcasted_iota(jnp.int32, (nrow, hd), 0)
    c_i = lax.broadcasted_iota(jnp.int32, (nrow, hd), 1)
    diag = (r_i // tq) == (c_i // MOBA_DH)
    qbd = jnp.where(diag, qt, 0.0)
    qbd_b = qbd.astype(BF16)
    gl = lax.broadcasted_iota(jnp.int32, (nrow, LANES), 1)

    def k_group(g, carry, next_src):
        gsum, pmax = carry
        nxt = g + ahead
        start_group(next_src, b, nxt if next_src is kc_ref else nxt - ngk, nxt)
        wait_group(kc_ref, g, g)
        for k in range(grp):
            j = g * grp + k
            kp = buf_ref[ring(g) + k].reshape(hd, PAGE_SIZE)
            s = _dot(qbd_b, kp.astype(BF16))
            sall_ref[j] = s
            gsum = gsum + jnp.where(gl == j // ppb, jnp.sum(s, -1, keepdims=True), 0.0)
            pmax = jnp.where(gl == j, jnp.max(s, -1, keepdims=True), pmax)
        return gsum, pmax

    carry = (jnp.zeros((nrow, LANES), F32), jnp.full((nrow, LANES), NEG_INF, F32))
    carry = lax.fori_loop(0, ngk - ahead, lambda g, c: k_group(g, c, kc_ref), carry)
    for g in range(ngk - ahead, ngk):
        carry = k_group(g, carry, vc_ref)
    gsum, pmax = carry

    gate = gsum * np.float32(1.0 / MOBA_BLOCK)
    gsel = jnp.where(gl < nfull, gate, NEG_INF)
    sel = jnp.zeros((nrow, LANES), F32)
    for _ in range(min(MOBA_TOPK, nfull)):
        m = jnp.max(gsel, -1, keepdims=True)
        idx = jnp.min(jnp.where(gsel == m, gl, LANES), -1, keepdims=True)
        hit = gl == idx
        sel = jnp.where(hit & (m > NEG_INF), 1.0, sel)
        gsel = jnp.where(hit, NEG_INF, gsel)

    kn = jnp.concatenate([kn_ref[...], jnp.zeros((LANES - tq, hd), F32)], axis=0).astype(BF16)
    vn = jnp.concatenate([vn_ref[...], jnp.zeros((LANES - tq, hd), F32)], axis=0).astype(BF16)
    s_new = _dot_nt(qbd_b, kn)
    nr = lax.broadcasted_iota(jnp.int32, (nrow, LANES), 0)
    s_new = jnp.where(gl <= (nr % tq), s_new, NEG_INF)

    eb_r = lax.broadcasted_iota(jnp.int32, (LANES, LANES), 0)
    eb_c = lax.broadcasted_iota(jnp.int32, (LANES, LANES), 1)
    expand = jnp.where((eb_c // ppb == eb_r) & (eb_c < npages), 1.0, 0.0).astype(BF16)
    selp = _dot(sel.astype(BF16), expand)
    m_all = jnp.maximum(jnp.max(jnp.where(selp > 0.0, pmax, NEG_INF), -1, keepdims=True),
                        jnp.max(s_new, -1, keepdims=True))
    p_new = jnp.exp(s_new - m_all)
    l_new = jnp.sum(p_new, -1, keepdims=True)
    acc0 = _dot(p_new.astype(BF16), vn)


    def v_group(g, carry, prefetch):
        lsum, acc = carry
        if prefetch:
            start_group(vc_ref, b, g + ahead, ngk + g + ahead)
        else:
            @pl.when(b + 1 < nseq)
            def _():
                start_group(kc_ref, b + 1, g + ahead - ngk, ngk + g + ahead)
        wait_group(vc_ref, g, ngk + g)
        for k in range(grp):
            j = g * grp + k
            vp = buf_ref[ring(ngk + g) + k].reshape(hd, PAGE_SIZE).astype(BF16)
            col = jnp.max(jnp.where(gl == j, selp, 0.0), -1, keepdims=True)
            p = jnp.exp(jnp.where(col > 0.0, sall_ref[j] - m_all, NEG_INF))
            lsum = lsum + p
            acc = acc + _dot_nt(p.astype(BF16), vp)
        return lsum, acc

    carry = (jnp.zeros((nrow, LANES), F32), acc0)
    carry = lax.fori_loop(0, ngk - ahead, lambda g, c: v_group(g, c, True), carry)
    for g in range(ngk - ahead, ngk):
        carry = v_group(g, carry, False)
    lsum, acc = carry
    l_all = l_new + jnp.sum(lsum, -1, keepdims=True)
    res = jnp.where(diag, acc / l_all, 0.0).reshape(MOBA_HEADS, tq, hd)
    o_ref[...] = jnp.sum(res, axis=0)


def _moba_sample(h, page_table, cache_kt, cache_vt, layer, nseq, tq):
    npages = page_table.shape[1]
    hd = MOBA_W
    grp = _moba_group(npages)
    nslot = MOBA_RING * grp
    assert npages <= LANES and npages % (MOBA_BLOCK // PAGE_SIZE) == 0
    assert npages % grp == 0 and npages // grp >= MOBA_RING - 1
    grid_spec = pltpu.PrefetchScalarGridSpec(
        num_scalar_prefetch=1,
        grid=(nseq,),
        in_specs=[pl.BlockSpec((tq, hd), lambda b, pt: (b, COL_QB // 4)),
                  pl.BlockSpec((tq, hd), lambda b, pt: (b, COL_KB // 4)),
                  pl.BlockSpec((tq, hd), lambda b, pt: (b, COL_VB // 4)),
                  pl.BlockSpec(memory_space=pl.ANY),
                  pl.BlockSpec(memory_space=pl.ANY)],
        out_specs=pl.BlockSpec((tq, hd), lambda b, pt: (b, 0)),
        scratch_shapes=[pltpu.VMEM((nslot, MOBA_HEADS, MOBA_DH, PAGE_SIZE), F32),
                        pltpu.SemaphoreType.DMA((nslot,)),
                        pltpu.VMEM((npages, MOBA_HEADS * tq, PAGE_SIZE), F32)],
    )
    return pl.pallas_call(
        functools.partial(_moba_sample_kernel, layer=layer, npages=npages, tq=tq),
        grid_spec=grid_spec,
        out_shape=jax.ShapeDtypeStruct((nseq * tq, hd), F32),
        compiler_params=_cparams(("arbitrary",)),
        name="moba_sample",
    )(page_table, h, h, h, cache_kt, cache_vt)


def _sgu_kernel(u_ref, v_ref, g_ref, b_ref, w_ref, bias_ref, o_ref, *maybe_v_out, period):
    rows = u_ref.shape[0]
    u = _gelu_exact(u_ref[...])
    vn = _layer_norm(_gelu_exact(v_ref[...]), g_ref[...], b_ref[...])
    if maybe_v_out:
        maybe_v_out[0][...] = vn
    t = lax.broadcasted_iota(jnp.int32, (rows, rows), 0)
    s = lax.broadcasted_iota(jnp.int32, (rows, rows), 1)
    keep = ((t // period) == (s // period)) & ((s % period) <= (t % period))
    vb = vn.astype(BF16)
    mix = []
    for g in range(SG_GROUPS):
        wg = jnp.where(keep, w_ref[g], 0.0).astype(BF16)
        mix.append(_dot(wg, vb[:, g * SG_GC:(g + 1) * SG_GC]))
    o_ref[...] = u * (jnp.concatenate(mix, axis=1) + bias_ref[...])


def _sgu(h, ln_g, ln_b, wmix, bias, l, rows, period, want_v):
    n = h.shape[0]
    const = lambda shp: _layer_spec(shp, l)
    out_specs = [pl.BlockSpec((rows, SG_W), lambda i: (i, 0))]
    out_shape = [jax.ShapeDtypeStruct((n, SG_W), F32)]
    if want_v:
        out_specs.append(pl.BlockSpec((rows, SG_W), lambda i: (i, 0)))
        out_shape.append(jax.ShapeDtypeStruct((n, SG_W), F32))
    res = pl.pallas_call(
        functools.partial(_sgu_kernel, period=period),
        grid=(n // rows,),
        in_specs=[pl.BlockSpec((rows, SG_W), lambda i: (i, 6)),
                  pl.BlockSpec((rows, SG_W), lambda i: (i, 7)),
                  const((1, SG_W)), const((1, SG_W)),
                  const((SG_GROUPS, rows, rows)), const((rows, SG_W))],
        out_specs=out_specs,
        out_shape=out_shape,
        compiler_params=_cparams(("parallel",)),
        name="sgu",
    )(h, h, ln_g, ln_b, wmix, bias)
    return res if want_v else (res[0], None)


def _merge_kernel(oa_ref, ob_ref, oc_ref, ga_ref, gb_ref, gc_ref, x_ref, pa_ref, pb_ref, pc_ref, wo_ref,
                  g_ref, b_ref, o_ref):
    m = (jax.nn.sigmoid(ga_ref[...]) * _dot(oa_ref[...].astype(BF16), pa_ref[...])
         + jax.nn.sigmoid(gb_ref[...]) * _dot(ob_ref[...].astype(BF16), pb_ref[...])
         + jax.nn.sigmoid(gc_ref[...]) * _dot(oc_ref[...].astype(BF16), pc_ref[...]))
    y = np.float32(ALPHA) * x_ref[...] + _dot(m.astype(BF16), wo_ref[...])
    o_ref[...] = _layer_norm(y, g_ref[...], b_ref[...])


def _merge(oa, ob, oc, h, x, pa, pb, pc, wo, g, b, l):
    n = x.shape[0]
    tm = _row_tile(n, 512)
    row = lambda w, j: pl.BlockSpec((tm, w), lambda i: (i, j))
    const = lambda shp: _layer_spec(shp, l)
    return pl.pallas_call(
        _merge_kernel,
        grid=(n // tm,),
        in_specs=[row(GLA_V, 0), row(MOBA_W, 0), row(SG_W, 0),
                  row(D_MODEL, 4), row(D_MODEL, 5), row(D_MODEL, 6), row(D_MODEL, 0),
                  const((GLA_V, D_MODEL)), const((MOBA_W, D_MODEL)), const((SG_W, D_MODEL)),
                  const((D_MODEL, D_MODEL)), const((1, D_MODEL)), const((1, D_MODEL))],
        out_specs=row(D_MODEL, 0),
        out_shape=jax.ShapeDtypeStruct((n, D_MODEL), F32),
        compiler_params=_cparams(("parallel",)),
        name="merge",
    )(oa, ob, oc, h, h, h, x, pa, pb, pc, wo, g, b)


MOE_CHUNK_EXPERTS = 8


def _moe_kernel(x_ref, wr_ref, br_ref, wg_ref, wu_ref, wd_ref, sg_ref, su_ref, sd_ref, g_ref, b_ref,
                o_ref, xb_ref, gates_ref, acc_ref):
    c = pl.program_id(1)
    nchunk = pl.num_programs(1)
    tm = x_ref.shape[0]
    ce = MOE_CHUNK_EXPERTS
    gsz = N_EXPERTS // N_EXPERT_GROUPS

    @pl.when(c == 0)
    def _():
        x = x_ref[...]
        xb = x.astype(BF16)
        xb_ref[...] = xb
        wr = wr_ref[...]
        w_hi = wr.astype(BF16)
        w_lo = (wr - w_hi.astype(F32)).astype(BF16)
        x_lo = (x - xb.astype(F32)).astype(BF16)
        both = _dot_nt(jnp.concatenate([w_hi, w_lo], axis=0), xb)
        s = jax.nn.sigmoid(both[:N_EXPERTS] + both[N_EXPERTS:] + _dot_nt(w_hi, x_lo))
        sel = s + br_ref[...]
        sel3 = sel.reshape(N_EXPERT_GROUPS, gsz, tm)
        j3 = lax.broadcasted_iota(jnp.int32, sel3.shape, 1)
        m1 = jnp.max(sel3, axis=1, keepdims=True)
        i1 = jnp.min(jnp.where(sel3 == m1, j3, gsz), axis=1, keepdims=True)
        m2 = jnp.max(jnp.where(j3 == i1, NEG_INF, sel3), axis=1, keepdims=True)
        grp = (m1 + m2).reshape(N_EXPERT_GROUPS, tm)
        gi = lax.broadcasted_iota(jnp.int32, grp.shape, 0)
        gkeep = jnp.zeros(grp.shape, F32)
        for _ in range(TOPK_GROUPS):
            m = jnp.max(grp, axis=0, keepdims=True)
            idx = jnp.min(jnp.where(grp == m, gi, N_EXPERT_GROUPS), axis=0, keepdims=True)
            hit = gi == idx
            gkeep = jnp.where(hit, 1.0, gkeep)
            grp = jnp.where(hit, NEG_INF, grp)
        emask = jnp.broadcast_to(gkeep.reshape(N_EXPERT_GROUPS, 1, tm), sel3.shape).reshape(N_EXPERTS, tm)
        cand = jnp.where(emask > 0.0, sel, NEG_INF)
        ei = lax.broadcasted_iota(jnp.int32, cand.shape, 0)
        chosen = jnp.zeros(cand.shape, F32)
        for _ in range(TOP_K):
            m = jnp.max(cand, axis=0, keepdims=True)
            idx = jnp.min(jnp.where(cand == m, ei, N_EXPERTS), axis=0, keepdims=True)
            hit = ei == idx
            chosen = jnp.where(hit, 1.0, chosen)
            cand = jnp.where(hit, NEG_INF, cand)
        w = jnp.where(chosen > 0.0, s, 0.0)
        gates_ref[...] = (w / jnp.sum(w, axis=0, keepdims=True) * np.float32(ROUTED_SCALE)).reshape(
            N_EXPERTS // ce, ce, tm)
        hs = _silu(_dot(xb, sg_ref[...])) * _dot(xb, su_ref[...])
        acc_ref[...] = _dot(hs.astype(BF16), sd_ref[...])

    xb = xb_ref[...]
    hh = _silu(_dot(xb, wg_ref[...])) * _dot(xb, wu_ref[...])
    gt = jnp.concatenate([gates_ref[c], jnp.zeros((LANES - ce, tm), F32)], axis=0).T
    gexp = jnp.concatenate([jnp.broadcast_to(gt[:, j:j + 1], (tm, D_EXPERT)) for j in range(ce)], axis=1)
    acc_ref[...] += _dot((hh * gexp).astype(BF16), wd_ref[...])

    @pl.when(c == nchunk - 1)
    def _():
        o_ref[...] = _layer_norm(np.float32(ALPHA) * x_ref[...] + acc_ref[...], g_ref[...], b_ref[...])


def _moe(x, wr_t, br, wg, wu, wd, sg, su, sd, g, b, l):
    n = x.shape[0]
    tm = _row_tile(n, 1024)
    cw = MOE_CHUNK_EXPERTS * D_EXPERT
    nchunk = N_EXPERTS // MOE_CHUNK_EXPERTS
    const = lambda shp: _layer_spec(shp, l)
    return pl.pallas_call(
        _moe_kernel,
        grid=(n // tm, nchunk),
        in_specs=[pl.BlockSpec((tm, D_MODEL), lambda i, c: (i, 0)),
                  const((N_EXPERTS, D_MODEL)), const((N_EXPERTS, 1)),
                  _layer_spec((D_MODEL, cw), l, lambda i, c: (0, c)),
                  _layer_spec((D_MODEL, cw), l, lambda i, c: (0, c)),
                  _layer_spec((cw, D_MODEL), l, lambda i, c: (c, 0)),
                  const((D_MODEL, D_SHARED)), const((D_MODEL, D_SHARED)), const((D_SHARED, D_MODEL)),
                  const((1, D_MODEL)), const((1, D_MODEL))],
        out_specs=pl.BlockSpec((tm, D_MODEL), lambda i, c: (i, 0)),
        out_shape=jax.ShapeDtypeStruct((n, D_MODEL), F32),
        scratch_shapes=[pltpu.VMEM((tm, D_MODEL), BF16),
                        pltpu.VMEM((nchunk, MOE_CHUNK_EXPERTS, tm), F32),
                        pltpu.VMEM((tm, D_MODEL), F32)],
        compiler_params=_cparams(("parallel", "arbitrary")),
        name="moe",
    )(x, wr_t, br, wg, wu, wd, sg, su, sd, g, b)


def _prep_weights(w_in, gla_w_a2, gla_b_a2, gla_norm_g, sg_ln_g, sg_ln_b, sg_w, sg_b, w_branch_a, w_branch_b,
                  w_branch_c, w_out, ln1_g, ln1_b, ln2_g, ln2_b, moe_w_router, moe_b_router, moe_w_gate,
                  moe_w_up, moe_w_down, sh_w_gate, sh_w_up, sh_w_down, dec_seq):
    depth = w_in.shape[0]
    c_alr = 2 * GLA_QK + 2 * GLA_V
    w_perm = jnp.concatenate(
        [w_in[:, :, :c_alr], w_in[:, :, c_alr + GLA_RANK:], w_in[:, :, c_alr:c_alr + GLA_RANK],
         jnp.zeros((depth, D_MODEL, LANES - GLA_RANK), w_in.dtype)], axis=2).astype(BF16)
    w2 = jnp.concatenate([gla_w_a2, jnp.zeros((depth, LANES - GLA_RANK, GLA_QK), F32)], axis=1).astype(BF16)
    reps = PAGE_SIZE // dec_seq if dec_seq < SG_CHUNK else 1
    nrow_s = 256
    tile_s = nrow_s // dec_seq
    return dict(
        w_perm=w_perm,
        w2=w2, w2t=jnp.swapaxes(w2, 1, 2),
        brow=gla_b_a2[:, None, :], bcol=gla_b_a2[:, :, None],
        gla_g=gla_norm_g[:, None, :],
        sg_g=sg_ln_g[:, None, :], sg_b=sg_ln_b[:, None, :],
        sg_w_p=jnp.tile(sg_w, (1, 1, SG_CHUNKS_PER_STEP, SG_CHUNKS_PER_STEP)),
        sg_bias_p=jnp.tile(jnp.repeat(jnp.swapaxes(sg_b, 1, 2), SG_GC, axis=2), (1, SG_CHUNKS_PER_STEP, 1)),
        sg_w_s=jnp.tile(sg_w[:, :, :dec_seq, :dec_seq], (1, 1, tile_s, tile_s)),
        sg_bias_s=jnp.tile(jnp.repeat(jnp.swapaxes(sg_b[:, :, :dec_seq], 1, 2), SG_GC, axis=2), (1, tile_s, 1)),
        pa=w_branch_a.astype(BF16), pb=w_branch_b.astype(BF16), pc=w_branch_c.astype(BF16),
        wo=w_out.astype(BF16),
        ln1_g=ln1_g[:, None, :], ln1_b=ln1_b[:, None, :], ln2_g=ln2_g[:, None, :], ln2_b=ln2_b[:, None, :],
        wr_t=jnp.swapaxes(moe_w_router, 1, 2), br=moe_b_router[:, :, None],
        wg=jnp.transpose(moe_w_gate, (0, 2, 1, 3)).reshape(depth, D_MODEL, N_EXPERTS * D_EXPERT).astype(BF16),
        wu=jnp.transpose(moe_w_up, (0, 2, 1, 3)).reshape(depth, D_MODEL, N_EXPERTS * D_EXPERT).astype(BF16),
        wd=moe_w_down.reshape(depth, N_EXPERTS * D_EXPERT, D_MODEL).astype(BF16),
        sg=sh_w_gate.astype(BF16), su=sh_w_up.astype(BF16), sd=sh_w_down.astype(BF16),
    )


def _token_tail(x, h, oa, ob, oc, w, l):
    x1 = _merge(oa, ob, oc, h, x, w["pa"], w["pb"], w["pc"], w["wo"], w["ln1_g"], w["ln1_b"], l)
    return _moe(x1, w["wr_t"], w["br"], w["wg"], w["wu"], w["wd"], w["sg"], w["su"], w["sd"], w["ln2_g"],
                w["ln2_b"], l)


def kernel(x_prompt, x_sample, cache_k, cache_v, state_gla, page_table, w_in, gla_w_a2, gla_b_a2, gla_norm_g,
           sg_ln_g, sg_ln_b, sg_w, sg_b, w_branch_a, w_branch_b, w_branch_c, w_out, ln1_g, ln1_b, ln2_g, ln2_b,
           moe_w_router, moe_b_router, moe_w_gate, moe_w_up, moe_w_down, sh_w_gate, sh_w_up, sh_w_down):
    bp, sp, d = x_prompt.shape
    db, t, _ = x_sample.shape
    depth = w_in.shape[0]
    assert d == D_MODEL and sp % MOBA_BLOCK == 0 and sp % SG_CHUNK == 0 and t <= SUBLANES
    assert (db * t) % 256 == 0 or db * t == 256
    w = _prep_weights(w_in, gla_w_a2, gla_b_a2, gla_norm_g, sg_ln_g, sg_ln_b, sg_w, sg_b, w_branch_a,
                      w_branch_b, w_branch_c, w_out, ln1_g, ln1_b, ln2_g, ln2_b, moe_w_router, moe_b_router,
                      moe_w_gate, moe_w_up, moe_w_down, sh_w_gate, sh_w_up, sh_w_down, t)
    cache_kt = jnp.swapaxes(cache_k, 3, 4)
    cache_vt = jnp.swapaxes(cache_v, 3, 4)
    xp = x_prompt.reshape(bp * sp, d)
    xs = x_sample.reshape(db * t, d)
    zero_state = jnp.zeros((bp, GLA_HEADS, GLA_DK, GLA_DV), F32)
    gla_p, gla_s, kp_l, vp_l, ks_l, vs_l, sgu_l = [], [], [], [], [], [], []
    for l in range(depth):
        gla_args = (w["w2"][l], w["w2t"][l], w["brow"][l], w["bcol"][l], w["gla_g"][l])
        hp = _inproj(xp, w["w_perm"], l)
        oa, s_p = _gla(hp, zero_state, *gla_args, bp, sp, math.gcd(bp, GLA_SEQS_PER_STEP))
        ob, kt, vt = _moba_prompt(hp, bp, sp)
        oc, _ = _sgu(hp, w["sg_g"], w["sg_b"], w["sg_w_p"], w["sg_bias_p"], l,
                     SG_CHUNKS_PER_STEP * SG_CHUNK, SG_CHUNK, False)
        xp = _token_tail(xp, hp, oa, ob, oc, w, l)
        hs = _inproj(xs, w["w_perm"], l)
        oa, s_s = _gla(hs, state_gla[l], *gla_args, db, t, math.gcd(db, GLA_SEQS_PER_STEP))
        ob = _moba_sample(hs, page_table, cache_kt, cache_vt, l, db, t)
        oc, vcs = _sgu(hs, w["sg_g"], w["sg_b"], w["sg_w_s"], w["sg_bias_s"], l, db * t, t, True)
        xs = _token_tail(xs, hs, oa, ob, oc, w, l)
        gla_p.append(s_p)
        gla_s.append(s_s)
        kp_l.append(jnp.swapaxes(kt, 3, 4))
        vp_l.append(jnp.swapaxes(vt, 3, 4))
        kb_s = hs[:, COL_KB * LANES:COL_KB * LANES + MOBA_W].reshape(db, t, MOBA_HEADS, MOBA_DH)
        vb_s = hs[:, COL_VB * LANES:COL_VB * LANES + MOBA_W].reshape(db, t, MOBA_HEADS, MOBA_DH)
        ks_l.append(kb_s.transpose(0, 2, 1, 3))
        vs_l.append(vb_s.transpose(0, 2, 1, 3))
        sgu_l.append(vcs.reshape(db, t, SG_W))
    return (xp.reshape(bp, sp, d), xs.reshape(db, t, d), jnp.stack(gla_p), jnp.stack(gla_s),
            jnp.stack(kp_l), jnp.stack(vp_l), jnp.stack(ks_l), jnp.stack(vs_l), jnp.stack(sgu_l))
```
